```python
import math
import jax, jax.numpy as jnp
from jax import lax
import numpy as np

D_MODEL = 2048
BATCH = 1
SEQ = 16384
DEPTH = 2

N_MEM = 256
HG_HEADS = 16
HG_KDIM = 128
HG_VDIM = 128
D_HG = HG_HEADS * HG_KDIM
D_HG_V = HG_HEADS * HG_VDIM
M2_HEADS = 32
M2_HEADDIM = 64
D_M2 = M2_HEADS * M2_HEADDIM
M2_GROUPS = 8
M2_DSTATE = 128
M2_CONV = 4
D_XBC = D_M2 + 2 * M2_GROUPS * M2_DSTATE
CHUNK = 64
XA_HEADS = 4
XA_HEADDIM = D_MODEL // XA_HEADS
D_FF = 5632
FFN_CONV = 3
NORM_EPS = 1e-6
M2_NORM_EPS = 1e-5
LB_FLOOR = 1e-30
IN_SPLITS = (D_HG, D_HG, D_HG_V, D_HG_V, D_M2, D_XBC, M2_HEADS, D_MODEL, D_MODEL)
N_IN = D_HG + D_HG + D_HG_V + D_HG_V + D_M2 + D_XBC + M2_HEADS + D_MODEL + D_MODEL

kernel_name = "hybrid_hgrn2_mamba2_gated_xattn_convglu"


def rms_norm(x, g, eps=NORM_EPS):
    xf = x.astype(jnp.float32)
    y = xf * lax.rsqrt(jnp.mean(xf * xf, axis=-1, keepdims=True) + eps)
    return (y * g.astype(jnp.float32)).astype(x.dtype)


def causal_dwconv(x, w, b):
    k_width, ch = w.shape
    y = lax.conv_general_dilated(x, w[:, None, :].astype(x.dtype), window_strides=(1,),
                                 padding=[(k_width - 1, 0)],
                                 dimension_numbers=("NWC", "WIO", "NWC"),
                                 feature_group_count=ch)
    return y + b.astype(x.dtype)


def to_chunks(t):
    b, s = t.shape[:2]
    return t.reshape((b, s // CHUNK, CHUNK) + t.shape[2:]).swapaxes(0, 1)


def from_chunks(t):
    t = t.swapaxes(0, 1)
    return t.reshape((t.shape[0], t.shape[1] * t.shape[2]) + t.shape[3:])


def masked_decay(diff):
    mask = jnp.tril(jnp.ones((CHUNK, CHUNK), dtype=bool))
    mask = mask.reshape((1, CHUNK, CHUNK) + (1,) * (diff.ndim - 3))
    return jnp.where(mask, jnp.exp(jnp.where(mask, diff, 0.0)), 0.0)


def gla_chunk_scan(q, k, v, log_f):
    bsz, _, h, kd = q.shape
    vd = v.shape[-1]

    def step(state, inp):
        qc, kc, vc, gc = inp
        b = jnp.cumsum(gc, axis=1)
        o_inter = jnp.einsum('bthk,bhkv->bthv', qc * jnp.exp(b), state)
        decay = masked_decay(b[:, :, None] - b[:, None, :])
        scores = jnp.einsum('bthk,btshk,bshk->bhts', qc, decay, kc)
        o_intra = jnp.einsum('bhts,bshv->bthv', scores, vc)
        b_last = b[:, -1]
        state = (jnp.exp(b_last)[..., None] * state
                 + jnp.einsum('bshk,bshv->bhkv', kc * jnp.exp(b_last[:, None] - b), vc))
        return state, o_inter + o_intra

    init = jnp.zeros((bsz, h, kd, vd), jnp.float32)
    _, o = lax.scan(step, init, (to_chunks(q), to_chunks(k), to_chunks(v), to_chunks(log_f)))
    return from_chunks(o)


def ssd_chunk_scan(xdt, a, bm, cm):
    bsz, _, g, hg, p = xdt.shape
    n = bm.shape[-1]

    def step(state, inp):
        xc, ac, bc, cc = inp
        cum = jnp.cumsum(ac, axis=1)
        lmat = masked_decay(cum[:, :, None] - cum[:, None, :])
        cb = jnp.einsum('btgn,bsgn->btsg', cc, bc)
        y_intra = jnp.einsum('btsg,btsgh,bsghp->btghp', cb, lmat, xc)
        y_inter = jnp.einsum('btgn,bghpn->btghp', cc, state) * jnp.exp(cum)[..., None]
        last = cum[:, -1]
        state = (jnp.exp(last)[..., None, None] * state
                 + jnp.einsum('bsgn,bsgh,bsghp->bghpn', bc, jnp.exp(last[:, None] - cum), xc))
        return state, y_intra + y_inter

    init = jnp.zeros((bsz, g, hg, p, n), jnp.float32)
    _, y = lax.scan(step, init, (to_chunks(xdt), to_chunks(a), to_chunks(bm), to_chunks(cm)))
    return from_chunks(y)


def hgrn2_branch(q_raw, f_raw, i_raw, og_raw, lb, norm_g):
    bsz, s, _ = q_raw.shape
    f32 = jnp.float32
    q = q_raw.astype(f32).reshape(bsz, s, HG_HEADS, HG_KDIM) * (HG_KDIM ** -0.5)
    z = f_raw.astype(f32).reshape(bsz, s, HG_HEADS, HG_KDIM)
    lb = lb.astype(f32).reshape(HG_HEADS, HG_KDIM)
    log_f = jnp.logaddexp(jax.nn.log_sigmoid(z),
                          jnp.log(jnp.maximum(lb, LB_FLOOR)) + jax.nn.log_sigmoid(-z))
    k = (1.0 - lb) * jax.nn.sigmoid(-z)
    v = i_raw.astype(f32).reshape(bsz, s, HG_HEADS, HG_VDIM)
    o = gla_chunk_scan(q, k, v, log_f)
    o = o * jax.nn.sigmoid(og_raw.astype(f32).reshape(bsz, s, HG_HEADS, HG_VDIM))
    o = o * lax.rsqrt(jnp.mean(o * o, axis=-1, keepdims=True) + NORM_EPS)
    o = o * norm_g.astype(f32).reshape(HG_HEADS, HG_VDIM)
    return o.reshape(bsz, s, D_HG_V).astype(q_raw.dtype)


def mamba2_branch(z, xbc_raw, dt_raw, conv_w, conv_b, dt_bias, a_log, d_skip, norm_g):
    bsz, s, _ = z.shape
    f32 = jnp.float32
    xbc = jax.nn.silu(causal_dwconv(xbc_raw, conv_w, conv_b))
    xs, bm, cm = jnp.split(xbc, [D_M2, D_M2 + M2_GROUPS * M2_DSTATE], axis=-1)
    hpg = M2_HEADS // M2_GROUPS
    xs = xs.astype(f32).reshape(bsz, s, M2_GROUPS, hpg, M2_HEADDIM)
    bm = bm.astype(f32).reshape(bsz, s, M2_GROUPS, M2_DSTATE)
    cm = cm.astype(f32).reshape(bsz, s, M2_GROUPS, M2_DSTATE)
    dt = jax.nn.softplus(dt_raw.astype(f32) + dt_bias.astype(f32))
    dt = dt.reshape(bsz, s, M2_GROUPS, hpg)
    a_neg = -jnp.exp(a_log.astype(f32)).reshape(M2_GROUPS, hpg)
    y = ssd_chunk_scan(xs * dt[..., None], dt * a_neg, bm, cm)
    y = y + d_skip.astype(f32).reshape(M2_GROUPS, hpg)[:, :, None] * xs
    y = y.reshape(bsz, s, D_M2) * jax.nn.silu(z.astype(f32))
    yg = y.reshape(bsz, s, M2_GROUPS, D_M2 // M2_GROUPS)
    yg = yg * lax.rsqrt(jnp.mean(yg * yg, axis=-1, keepdims=True) + M2_NORM_EPS)
    y = yg.reshape(bsz, s, D_M2) * norm_g.astype(f32)
    return y.astype(z.dtype)


def token_mixer(u, w_in, lb, hg_norm_g, m2_conv_w, m2_conv_b, m2_dt_bias, m2_a_log, m2_d,
                m2_norm_g, w_branch_hg, w_branch_m2, w_out):
    proj = u @ w_in
    idx = [int(i) for i in np.cumsum(IN_SPLITS)[:-1]]
    hq, hf, hi, hog, mz, mxbc, mdt, g_hg, g_m2 = jnp.split(proj, idx, axis=-1)
    y_hg = hgrn2_branch(hq, hf, hi, hog, lb, hg_norm_g) @ w_branch_hg
    y_m2 = mamba2_branch(mz, mxbc, mdt, m2_conv_w, m2_conv_b, m2_dt_bias, m2_a_log, m2_d,
                         m2_norm_g) @ w_branch_m2
    merged = jax.nn.sigmoid(g_hg) * y_hg + jax.nn.sigmoid(g_m2) * y_m2
    return merged @ w_out


def memory_cross_attention(u, memn, wq, wkv, wo):
    bsz, s, _ = u.shape
    q = (u @ wq).reshape(bsz, s, XA_HEADS, XA_HEADDIM)
    k, v = jnp.split(memn @ wkv, 2, axis=-1)
    k = k.reshape(bsz, -1, XA_HEADS, XA_HEADDIM)
    v = v.reshape(bsz, -1, XA_HEADS, XA_HEADDIM)
    scores = jnp.einsum('bshd,bmhd->bhsm', q, k).astype(jnp.float32) * (XA_HEADDIM ** -0.5)
    probs = jax.nn.softmax(scores, axis=-1).astype(v.dtype)
    o = jnp.einsum('bhsm,bmhd->bshd', probs, v).reshape(bsz, s, D_MODEL)
    return o @ wo


def conv_glu_ffn(u, w_up, conv_w, conv_b, w_down):
    gate, up = jnp.split(u @ w_up, 2, axis=-1)
    gate = causal_dwconv(gate, conv_w, conv_b)
    return (jax.nn.gelu(gate, approximate=False) * up) @ w_down


def setup_inputs(seed: int = 0) -> dict:
    key = jax.random.key(seed)
    ks = jax.random.split(key, 32)
    L = DEPTH
    f32 = jnp.float32

    def nrm(k, shape, scale):
        return jax.random.normal(k, shape, f32) * scale

    dt0 = jnp.exp(jax.random.uniform(ks[8], (L, M2_HEADS), f32) * (math.log(0.1) - math.log(0.001))
                  + math.log(0.001))
    return {
        "x": nrm(ks[0], (BATCH, SEQ, D_MODEL), 1.0),
        "mem": nrm(ks[1], (BATCH, N_MEM, D_MODEL), 1.0),
        "mix_norm_g": 1.0 + nrm(ks[2], (L, D_MODEL), 0.02),
        "w_in": nrm(ks[3], (L, D_MODEL, N_IN), D_MODEL ** -0.5),
        "hg_lb_logits": 1.0 + nrm(ks[4], (L, D_HG), 0.3),
        "hg_norm_g": 1.0 + nrm(ks[5], (L, D_HG_V), 0.02),
        "m2_conv_w": nrm(ks[6], (L, M2_CONV, D_XBC), M2_CONV ** -0.5),
        "m2_conv_b": nrm(ks[7], (L, D_XBC), 0.02),
        "m2_dt_bias": dt0 + jnp.log(-jnp.expm1(-dt0)),
        "m2_A_log": jnp.log(jax.random.uniform(ks[9], (L, M2_HEADS), f32, 1.0, 16.0)),
        "m2_D": 1.0 + nrm(ks[10], (L, M2_HEADS), 0.1),
        "m2_norm_g": 1.0 + nrm(ks[11], (L, D_M2), 0.02),
        "w_branch_hg": nrm(ks[12], (L, D_HG_V, D_MODEL), D_HG_V ** -0.5),
        "w_branch_m2": nrm(ks[13], (L, D_M2, D_MODEL), D_M2 ** -0.5),
        "w_out": nrm(ks[14], (L, D_MODEL, D_MODEL), D_MODEL ** -0.5),
        "mem_norm_g": 1.0 + nrm(ks[15], (D_MODEL,), 0.02),
        "xa_norm_g": 1.0 + nrm(ks[16], (L, D_MODEL), 0.02),
        "xa_wq": nrm(ks[17], (L, D_MODEL, D_MODEL), D_MODEL ** -0.5),
        "xa_wkv": nrm(ks[18], (L, D_MODEL, 2 * D_MODEL), D_MODEL ** -0.5),
        "xa_wo": nrm(ks[19], (L, D_MODEL, D_MODEL), D_MODEL ** -0.5),
        "ffn_norm_g": 1.0 + nrm(ks[20], (L, D_MODEL), 0.02),
        "ffn_w_up": nrm(ks[21], (L, D_MODEL, 2 * D_FF), D_MODEL ** -0.5),
        "ffn_conv_w": nrm(ks[22], (L, FFN_CONV, D_FF), FFN_CONV ** -0.5),
        "ffn_conv_b": nrm(ks[23], (L, D_FF), 0.02),
        "ffn_w_down": nrm(ks[24], (L, D_FF, D_MODEL), D_FF ** -0.5),
        "final_norm_g": 1.0 + nrm(ks[25], (D_MODEL,), 0.02),
    }


def reference(x, mem, mix_norm_g, w_in, hg_lb_logits, hg_norm_g, m2_conv_w, m2_conv_b,
              m2_dt_bias, m2_A_log, m2_D, m2_norm_g, w_branch_hg, w_branch_m2, w_out,
              mem_norm_g, xa_norm_g, xa_wq, xa_wkv, xa_wo, ffn_norm_g, ffn_w_up, ffn_conv_w,
              ffn_conv_b, ffn_w_down, final_norm_g):
    p = jax.nn.softmax(hg_lb_logits.astype(jnp.float32), axis=0)
    lower_bounds = jnp.cumsum(p, axis=0) - p[0]
    memn = rms_norm(mem, mem_norm_g)
    h = x
    for l in range(DEPTH):
        u = rms_norm(h, mix_norm_g[l])
        h = h + token_mixer(u, w_in[l], lower_bounds[l], hg_norm_g[l], m2_conv_w[l],
                            m2_conv_b[l], m2_dt_bias[l], m2_A_log[l], m2_D[l], m2_norm_g[l],
                            w_branch_hg[l], w_branch_m2[l], w_out[l])
        u = rms_norm(h, xa_norm_g[l])
        h = h + memory_cross_attention(u, memn, xa_wq[l], xa_wkv[l], xa_wo[l])
        u = rms_norm(h, ffn_norm_g[l])
        h = h + conv_glu_ffn(u, ffn_w_up[l], ffn_conv_w[l], ffn_conv_b[l], ffn_w_down[l])
    return rms_norm(h, final_norm_g)
```

```python
import functools
import math

import jax
import jax.numpy as jnp
from jax import lax
from jax.experimental import pallas as pl
from jax.experimental.pallas import tpu as pltpu

F32 = jnp.float32
BF16 = jnp.bfloat16

D_MODEL = 2048
HG_HEADS = 16
HG_KDIM = 128
HG_VDIM = 128
D_HG = HG_HEADS * HG_KDIM
M2_HEADS = 32
M2_HEADDIM = 64
D_M2 = M2_HEADS * M2_HEADDIM
M2_GROUPS = 8
M2_DSTATE = 128
M2_CONV = 4
M2_GROUP_W = D_M2 // M2_GROUPS
M2_HPG = M2_HEADS // M2_GROUPS
CHUNK = 64
HG_SUB = 32
XA_HEADS = 4
XA_HEADDIM = D_MODEL // XA_HEADS
D_FF = 5632
FFN_CONV = 3
NORM_EPS = 1e-6
M2_NORM_EPS = 1e-5
LB_FLOOR = 1e-30
HG_SAFE_DECAY = 80.0

OFF_HQ = 0
OFF_HF = OFF_HQ + D_HG
OFF_HI = OFF_HF + D_HG
OFF_HOG = OFF_HI + D_HG
OFF_MZ = OFF_HOG + D_HG
OFF_MX = OFF_MZ + D_M2
OFF_MB = OFF_MX + D_M2
OFF_MC = OFF_MB + M2_GROUPS * M2_DSTATE
OFF_GHG = OFF_MC + M2_GROUPS * M2_DSTATE
OFF_GM2 = OFF_GHG + D_MODEL
N_PROJ = OFF_GM2 + D_MODEL
DT_PAD = 128

LANES = 128
HALO = 8
VMEM_LIMIT = 56 * 1024 * 1024


def _cparams(n_axes):
    return pltpu.CompilerParams(
        dimension_semantics=("arbitrary",) * n_axes,
        vmem_limit_bytes=VMEM_LIMIT)


def _rms(x, g, eps):
    return x * lax.rsqrt(jnp.mean(x * x, axis=-1, keepdims=True) + eps) * g


def _sigmoid(x):
    return 1.0 / (1.0 + jnp.exp(-x))


def _softplus(x):
    return jnp.maximum(x, 0.0) + jnp.log1p(jnp.exp(-jnp.abs(x)))


def _dot(a, b):
    return jnp.dot(a, b, preferred_element_type=F32)


def _dot_nt(a, b):
    return lax.dot_general(a, b, (((1,), (1,)), ((), ())), preferred_element_type=F32)


def _dot_tn(a, b):
    return lax.dot_general(a, b, (((0,), (0,)), ((), ())), preferred_element_type=F32)


def _split3(x):
    hi = x.astype(BF16)
    r1 = x - hi.astype(F32)
    mid = r1.astype(BF16)
    lo = (r1 - mid.astype(F32)).astype(BF16)
    return hi, mid, lo


def _seg_cumsum(x, seg):
    row = lax.broadcasted_iota(jnp.int32, x.shape, 0) % seg
    k = 1
    while k < seg:
        x = x + jnp.where(row >= k, pltpu.roll(x, k, axis=0), 0.0)
        k *= 2
    return x


def _norm_matmul_kernel(a_ref, g_ref, b_ref, o_ref, an_ref):
    @pl.when(pl.program_id(1) == 0)
    def _():
        an_ref[...] = _rms(a_ref[...], g_ref[...], NORM_EPS).astype(BF16)

    o_ref[...] = _dot(an_ref[...], b_ref[...]).astype(o_ref.dtype)


def norm_matmul(a, g, b, *, out_dtype, tm=1024, tn=1024):
    m, k = a.shape
    n = b.shape[1]
    tm = min(tm, m)
    tn = min(tn, n)
    return pl.pallas_call(
        _norm_matmul_kernel,
        grid=(m // tm, n // tn),
        in_specs=[
            pl.BlockSpec((tm, k), lambda i, j: (i, 0)),
            pl.BlockSpec((1, k), lambda i, j: (0, 0)),
            pl.BlockSpec((k, tn), lambda i, j: (0, j)),
        ],
        out_specs=pl.BlockSpec((tm, tn), lambda i, j: (i, j)),
        out_shape=jax.ShapeDtypeStruct((m, n), out_dtype),
        scratch_shapes=[pltpu.VMEM((tm, k), BF16)],
        compiler_params=_cparams(2),
        name="norm_matmul",
    )(a, g.reshape(1, k), b)


def _matmul_res_kernel(a_ref, b_ref, r_ref, o_ref):
    o_ref[...] = r_ref[...] + _dot(a_ref[...], b_ref[...])


def matmul_residual(a, b, r, *, tm=512, tn=512):
    m, k = a.shape
    n = b.shape[1]
    tm = min(tm, m)
    tn = min(tn, n)
    return pl.pallas_call(
        _matmul_res_kernel,
        grid=(m // tm, n // tn),
        in_specs=[
            pl.BlockSpec((tm, k), lambda i, j: (i, 0)),
            pl.BlockSpec((k, tn), lambda i, j: (0, j)),
            pl.BlockSpec((tm, tn), lambda i, j: (i, j)),
        ],
        out_specs=pl.BlockSpec((tm, tn), lambda i, j: (i, j)),
        out_shape=jax.ShapeDtypeStruct((m, n), F32),
        compiler_params=_cparams(2),
        name="matmul_residual",
    )(a, b, r)


def _merge_kernel(a1_ref, b1_ref, a2_ref, b2_ref, g1_ref, g2_ref, o_ref):
    y1 = _dot(a1_ref[...], b1_ref[...])
    y2 = _dot(a2_ref[...], b2_ref[...])
    o_ref[...] = (_sigmoid(g1_ref[...]) * y1 + _sigmoid(g2_ref[...]) * y2).astype(o_ref.dtype)


def merge_branches(o_hg, w_hg, o_m2, w_m2, proj, *, tm=512, tn=512):
    m, k = o_hg.shape
    n = w_hg.shape[1]
    tm = min(tm, m)
    g1 = OFF_GHG // tn
    g2 = OFF_GM2 // tn
    return pl.pallas_call(
        _merge_kernel,
        grid=(m // tm, n // tn),
        in_specs=[
            pl.BlockSpec((tm, k), lambda i, j: (i, 0)),
            pl.BlockSpec((k, tn), lambda i, j: (0, j)),
            pl.BlockSpec((tm, k), lambda i, j: (i, 0)),
            pl.BlockSpec((k, tn), lambda i, j: (0, j)),
            pl.BlockSpec((tm, tn), lambda i, j: (i, g1 + j)),
            pl.BlockSpec((tm, tn), lambda i, j: (i, g2 + j)),
        ],
        out_specs=pl.BlockSpec((tm, tn), lambda i, j: (i, j)),
        out_shape=jax.ShapeDtypeStruct((m, n), BF16),
        compiler_params=_cparams(2),
        name="merge_branches",
    )(o_hg, w_hg, o_m2, w_m2, proj, proj)


def _hgrn2_kernel(q_ref, f_ref, i_ref, og_ref, lbl_ref, ng_ref, o_ref, st_ref, oacc_ref,
                  *, layer, tb):
    @pl.when(pl.program_id(1) == 0)
    def _():
        st_ref[...] = jnp.zeros_like(st_ref)

    logits = lbl_ref[...]
    e = jnp.exp(logits - jnp.max(logits, axis=0, keepdims=True))
    p = e / jnp.sum(e, axis=0, keepdims=True)
    lb = jnp.zeros((1, HG_KDIM), F32)
    for i in range(1, layer + 1):
        lb = lb + p[i:i + 1, :]

    z = f_ref[...]
    ls = jnp.minimum(z, 0.0) - jnp.log1p(jnp.exp(-jnp.abs(z)))
    c = jnp.log(jnp.maximum(lb, LB_FLOOR)) + (ls - z)
    log_f = jnp.maximum(ls, c) + jnp.log1p(jnp.exp(-jnp.abs(ls - c)))
    kk = (1.0 - lb) * _sigmoid(-z)
    q = q_ref[...] * (HG_KDIM ** -0.5)
    v = i_ref[...].astype(BF16)

    bl = _seg_cumsum(log_f, HG_SUB)
    n_sub = tb // HG_SUB
    qt = (q * jnp.exp(bl)).astype(BF16)
    worst = jnp.max(-bl)

    tri = (lax.broadcasted_iota(jnp.int32, (HG_SUB, HG_SUB), 0)
           >= lax.broadcasted_iota(jnp.int32, (HG_SUB, HG_SUB), 1))
    col_id = lax.broadcasted_iota(jnp.int32, (HG_SUB, HG_SUB), 1)

    def run(fast):
        if fast:
            kinc = (kk * jnp.exp(-bl)).astype(BF16)
        for cix in range(n_sub):
            lo = cix * HG_SUB
            hi = lo + HG_SUB
            blc = bl[lo:hi, :]
            b_last = blc[HG_SUB - 1:HG_SUB, :]
            qc = qt[lo:hi, :]
            vc = v[lo:hi, :]
            kc = kk[lo:hi, :]
            kdec = (kc * jnp.exp(b_last - blc)).astype(BF16)
            st = st_ref[...]
            o_inter = _dot_nt(qc, st.astype(BF16))
            if fast:
                sc = _dot_nt(qc, kinc[lo:hi, :])
            else:
                qraw = q[lo:hi, :]
                sc = jnp.zeros((HG_SUB, HG_SUB), F32)
                for s in range(HG_SUB):
                    d = jnp.exp(jnp.minimum(blc - blc[s:s + 1, :], 0.0))
                    col = jnp.sum(qraw * d * kc[s:s + 1, :], axis=-1, keepdims=True)
                    sc = jnp.where(col_id == s, col, sc)
            sc = jnp.where(tri, sc, 0.0)
            o_intra = _dot(sc.astype(BF16), vc)
            oacc_ref[lo:hi, :] = o_inter + o_intra
            st_ref[...] = st * jnp.exp(b_last) + _dot_tn(vc, kdec)

    @pl.when(worst <= HG_SAFE_DECAY)
    def _():
        run(True)

    @pl.when(worst > HG_SAFE_DECAY)
    def _():
        run(False)

    o = oacc_ref[...] * _sigmoid(og_ref[...])
    o_ref[...] = _rms(o, ng_ref[...], NORM_EPS).astype(o_ref.dtype)


def hgrn2_branch(proj, lb_logits, norm_g, layer, *, tb=512):
    s = proj.shape[0]
    tb = min(tb, s)
    depth = lb_logits.shape[0]
    kern = functools.partial(_hgrn2_kernel, layer=layer, tb=tb)

    def col(off):
        base = off // HG_KDIM
        return pl.BlockSpec((tb, HG_KDIM), lambda h, i: (i, base + h))

    return pl.pallas_call(
        kern,
        grid=(HG_HEADS, s // tb),
        in_specs=[
            col(OFF_HQ), col(OFF_HF), col(OFF_HI), col(OFF_HOG),
            pl.BlockSpec((depth, HG_KDIM), lambda h, i: (0, h)),
            pl.BlockSpec((1, HG_VDIM), lambda h, i: (0, h)),
        ],
        out_specs=pl.BlockSpec((tb, HG_VDIM), lambda h, i: (i, h)),
        out_shape=jax.ShapeDtypeStruct((s, D_HG), BF16),
        scratch_shapes=[pltpu.VMEM((HG_VDIM, HG_KDIM), F32),
                        pltpu.VMEM((tb, HG_VDIM), F32)],
        compiler_params=_cparams(2),
        name="hgrn2_branch",
    )(proj, proj, proj, proj, lb_logits, norm_g.reshape(1, D_HG))


def _mamba2_kernel(z_ref, x_ref, b_ref, c_ref, dt_ref, wx_ref, wb_ref, wc_ref,
                   bx_ref, bb_ref, bc_ref, dtb_ref, alog_ref, dsk_ref, ng_ref,
                   o_ref, ext_ref, st_ref, y_ref, *, tb):
    g = pl.program_id(0)
    w_all = M2_GROUP_W + 2 * M2_DSTATE

    @pl.when(pl.program_id(1) == 0)
    def _():
        st_ref[...] = jnp.zeros_like(st_ref)
        ext_ref[0:HALO, :] = jnp.zeros((HALO, w_all), F32)

    ext_ref[HALO:HALO + tb, 0:M2_GROUP_W] = x_ref[...]
    ext_ref[HALO:HALO + tb, M2_GROUP_W:M2_GROUP_W + M2_DSTATE] = b_ref[...]
    ext_ref[HALO:HALO + tb, M2_GROUP_W + M2_DSTATE:w_all] = c_ref[...]
    w = jnp.concatenate([wx_ref[...], wb_ref[...], wc_ref[...]], axis=1)
    bias = jnp.concatenate([bx_ref[...], bb_ref[...], bc_ref[...]], axis=1)
    acc = bias + w[M2_CONV - 1:M2_CONV, :] * ext_ref[HALO:HALO + tb, :]
    for j in range(1, M2_CONV):
        acc = acc + w[M2_CONV - 1 - j:M2_CONV - j, :] * ext_ref[HALO - j:HALO - j + tb, :]
    ext_ref[0:HALO, :] = ext_ref[tb:tb + HALO, :]
    xbc = acc * _sigmoid(acc)
    xs = xbc[:, 0:M2_GROUP_W]
    bm = xbc[:, M2_GROUP_W:M2_GROUP_W + M2_DSTATE].astype(BF16)
    cm = xbc[:, M2_GROUP_W + M2_DSTATE:w_all].astype(BF16)

    r_id = lax.broadcasted_iota(jnp.int32, (DT_PAD, M2_GROUP_W), 0)
    c_id = lax.broadcasted_iota(jnp.int32, (DT_PAD, M2_GROUP_W), 1)
    expand = (r_id == g * M2_HPG + c_id // M2_HEADDIM).astype(BF16)
    d_hi, d_mid, d_lo = _split3(dt_ref[...])
    dt_raw = _dot(d_hi, expand) + _dot(d_mid, expand) + _dot(d_lo, expand)
    dt = _softplus(dt_raw + dtb_ref[...])
    a = dt * (-jnp.exp(alog_ref[...]))
    cum = _seg_cumsum(a, CHUNK)
    xdt = xs * dt

    sr = lax.broadcasted_iota(jnp.int32, (HALO, M2_GROUP_W), 0)
    sc_ = lax.broadcasted_iota(jnp.int32, (HALO, M2_GROUP_W), 1)
    sel = (sc_ == sr * M2_HEADDIM).astype(BF16)
    tri = (lax.broadcasted_iota(jnp.int32, (CHUNK, CHUNK), 0)
           >= lax.broadcasted_iota(jnp.int32, (CHUNK, CHUNK), 1))

    for cix in range(tb // CHUNK):
        lo = cix * CHUNK
        hi = lo + CHUNK
        cumc = cum[lo:hi, :]
        xdtc = xdt[lo:hi, :]
        bc = bm[lo:hi, :]
        cc = cm[lo:hi, :]
        cb = _dot_nt(cc, bc)
        c_hi, c_mid, c_lo = _split3(cumc)
        rows = _dot_nt(sel, c_hi) + _dot_nt(sel, c_mid) + _dot_nt(sel, c_lo)
        for h in range(M2_HPG):
            cl = h * M2_HEADDIM
            ch = cl + M2_HEADDIM
            diff = cumc[:, cl:ch] - rows[h:h + 1, :]
            lmat = jnp.where(tri, jnp.exp(jnp.where(tri, diff, 0.0)), 0.0)
            y_ref[lo:hi, cl:ch] = _dot((cb * lmat).astype(BF16), xdtc[:, cl:ch].astype(BF16))
        st = st_ref[...]
        y_inter = _dot(cc, st.astype(BF16)) * jnp.exp(cumc)
        y_ref[lo:hi, :] = y_ref[lo:hi, :] + y_inter
        last = cumc[CHUNK - 1:CHUNK, :]
        wdec = (xdtc * jnp.exp(last - cumc)).astype(BF16)
        st_ref[...] = st * jnp.exp(last) + _dot_tn(bc, wdec)

    z = z_ref[...]
    y = (y_ref[...] + dsk_ref[...] * xs) * (z * _sigmoid(z))
    o_ref[...] = _rms(y, ng_ref[...], M2_NORM_EPS).astype(o_ref.dtype)


def mamba2_branch(proj, dt_raw, conv_w, conv_b, dt_bias_x, a_log_x, d_x, norm_g, *, tb=512):
    s = proj.shape[0]
    tb = min(tb, s)
    kern = functools.partial(_mamba2_kernel, tb=tb)
    gw = M2_GROUP_W
    ns = M2_DSTATE
    w_all = gw + 2 * ns

    def colw(off, width):
        base = off // width
        return pl.BlockSpec((tb, width), lambda g, i: (i, base + g))

    def par(rows, off, width):
        base = off // width
        return pl.BlockSpec((rows, width), lambda g, i: (0, base + g))

    conv_b = conv_b.reshape(1, -1)
    return pl.pallas_call(
        kern,
        grid=(M2_GROUPS, s // tb),
        in_specs=[
            colw(OFF_MZ, gw), colw(OFF_MX, gw), colw(OFF_MB, ns), colw(OFF_MC, ns),
            pl.BlockSpec((tb, DT_PAD), lambda g, i: (i, 0)),
            par(M2_CONV, 0, gw), par(M2_CONV, D_M2, ns), par(M2_CONV, D_M2 + M2_GROUPS * ns, ns),
            par(1, 0, gw), par(1, D_M2, ns), par(1, D_M2 + M2_GROUPS * ns, ns),
            par(1, 0, gw), par(1, 0, gw), par(1, 0, gw), par(1, 0, gw),
        ],
        out_specs=pl.BlockSpec((tb, gw), lambda g, i: (i, g)),
        out_shape=jax.ShapeDtypeStruct((s, D_M2), BF16),
        scratch_shapes=[pltpu.VMEM((tb + HALO, w_all), F32),
                        pltpu.VMEM((M2_DSTATE, gw), F32),
                        pltpu.VMEM((tb, gw), F32)],
        compiler_params=_cparams(2),
        name="mamba2_branch",
    )(proj, proj, proj, proj, dt_raw, conv_w, conv_w, conv_w, conv_b, conv_b, conv_b,
      dt_bias_x, a_log_x, d_x, norm_g.reshape(1, D_M2))


def _xattn_kernel(q_ref, k_ref, v_ref, o_ref):
    scale = XA_HEADDIM ** -0.5
    for h in range(XA_HEADS):
        lo = h * XA_HEADDIM
        hi = lo + XA_HEADDIM
        sc = _dot_nt(q_ref[:, lo:hi], k_ref[:, lo:hi]) * scale
        e = jnp.exp(sc - jnp.max(sc, axis=-1, keepdims=True))
        p = e / jnp.sum(e, axis=-1, keepdims=True)
        o_ref[:, lo:hi] = _dot(p.astype(BF16), v_ref[:, lo:hi]).astype(o_ref.dtype)


def cross_attention(q, kv, *, tm=512):
    s = q.shape[0]
    n_mem = kv.shape[0]
    tm = min(tm, s)
    return pl.pallas_call(
        _xattn_kernel,
        grid=(s // tm,),
        in_specs=[
            pl.BlockSpec((tm, D_MODEL), lambda i: (i, 0)),
            pl.BlockSpec((n_mem, D_MODEL), lambda i: (0, 0)),
            pl.BlockSpec((n_mem, D_MODEL), lambda i: (0, 1)),
        ],
        out_specs=pl.BlockSpec((tm, D_MODEL), lambda i: (i, 0)),
        out_shape=jax.ShapeDtypeStruct((s, D_MODEL), BF16),
        compiler_params=_cparams(1),
        name="cross_attention",
    )(q, kv, kv)


def _ffn_act_kernel(gate_ref, up_ref, w_ref, b_ref, o_ref, ext_ref, *, tb):
    @pl.when(pl.program_id(1) == 0)
    def _():
        ext_ref[0:HALO, :] = jnp.zeros((HALO, ext_ref.shape[1]), F32)

    ext_ref[HALO:HALO + tb, :] = gate_ref[...]
    w = w_ref[...]
    acc = b_ref[...] + w[FFN_CONV - 1:FFN_CONV, :] * ext_ref[HALO:HALO + tb, :]
    for j in range(1, FFN_CONV):
        acc = acc + w[FFN_CONV - 1 - j:FFN_CONV - j, :] * ext_ref[HALO - j:HALO - j + tb, :]
    ext_ref[0:HALO, :] = ext_ref[tb:tb + HALO, :]
    gelu = 0.5 * acc * (1.0 + lax.erf(acc * (2.0 ** -0.5)))
    o_ref[...] = (gelu * up_ref[...]).astype(o_ref.dtype)


def ffn_activation(up, conv_w, conv_b, *, tb=512, tn=512):
    s = up.shape[0]
    tb = min(tb, s)
    n_col = D_FF // tn
    kern = functools.partial(_ffn_act_kernel, tb=tb)
    return pl.pallas_call(
        kern,
        grid=(n_col, s // tb),
        in_specs=[
            pl.BlockSpec((tb, tn), lambda j, i: (i, j)),
            pl.BlockSpec((tb, tn), lambda j, i: (i, n_col + j)),
            pl.BlockSpec((FFN_CONV, tn), lambda j, i: (0, j)),
            pl.BlockSpec((1, tn), lambda j, i: (0, j)),
        ],
        out_specs=pl.BlockSpec((tb, tn), lambda j, i: (i, j)),
        out_shape=jax.ShapeDtypeStruct((s, D_FF), BF16),
        scratch_shapes=[pltpu.VMEM((tb + HALO, tn), F32)],
        compiler_params=_cparams(2),
        name="ffn_activation",
    )(up, up, conv_w, conv_b.reshape(1, D_FF))


def _final_norm_kernel(x_ref, g_ref, o_ref):
    o_ref[...] = _rms(x_ref[...], g_ref[...], NORM_EPS)


def final_norm(x, g, *, tm=512):
    m, k = x.shape
    tm = min(tm, m)
    return pl.pallas_call(
        _final_norm_kernel,
        grid=(m // tm,),
        in_specs=[pl.BlockSpec((tm, k), lambda i: (i, 0)),
                  pl.BlockSpec((1, k), lambda i: (0, 0))],
        out_specs=pl.BlockSpec((tm, k), lambda i: (i, 0)),
        out_shape=jax.ShapeDtypeStruct((m, k), F32),
        compiler_params=_cparams(1),
        name="final_norm",
    )(x, g.reshape(1, k))


def _repack_w_in(w):
    dt0 = OFF_MC + M2_GROUPS * M2_DSTATE
    main = jnp.concatenate([w[:, :dt0], w[:, dt0 + M2_HEADS:]], axis=1).astype(BF16)
    w_dt = jnp.pad(w[:, dt0:dt0 + M2_HEADS], ((0, 0), (0, DT_PAD - M2_HEADS))).astype(BF16)
    return main, w_dt


def _per_head_lanes(p):
    return jnp.repeat(p, M2_HEADDIM).reshape(1, D_M2)


def kernel(x, mem, mix_norm_g, w_in, hg_lb_logits, hg_norm_g, m2_conv_w, m2_conv_b, m2_dt_bias, m2_A_log, m2_D, m2_norm_g, w_branch_hg, w_branch_m2, w_out, mem_norm_g, xa_norm_g, xa_wq, xa_wkv, xa_wo, ffn_norm_g, ffn_w_up, ffn_conv_w, ffn_conv_b, ffn_w_down, final_norm_g):
    bsz, seq, d = x.shape
    depth = w_in.shape[0]
    outs = []
    for b in range(bsz):
        h = x[b]
        mem_b = mem[b]
        for l in range(depth):
            w_main, w_dt = _repack_w_in(w_in[l])
            proj = norm_matmul(h, mix_norm_g[l], w_main, out_dtype=F32)
            dt_raw = norm_matmul(h, mix_norm_g[l], w_dt, out_dtype=F32)
            o_hg = hgrn2_branch(proj, hg_lb_logits, hg_norm_g[l], l)
            o_m2 = mamba2_branch(proj, dt_raw, m2_conv_w[l], m2_conv_b[l],
                                 _per_head_lanes(m2_dt_bias[l]), _per_head_lanes(m2_A_log[l]),
                                 _per_head_lanes(m2_D[l]), m2_norm_g[l])
            merged = merge_branches(o_hg, w_branch_hg[l].astype(BF16),
                                    o_m2, w_branch_m2[l].astype(BF16), proj)
            h = matmul_residual(merged, w_out[l].astype(BF16), h)

            q = norm_matmul(h, xa_norm_g[l], xa_wq[l].astype(BF16), out_dtype=BF16)
            kv = norm_matmul(mem_b, mem_norm_g, xa_wkv[l].astype(BF16), out_dtype=BF16)
            att = cross_attention(q, kv)
            h = matmul_residual(att, xa_wo[l].astype(BF16), h)

            up = norm_matmul(h, ffn_norm_g[l], ffn_w_up[l].astype(BF16), out_dtype=F32)
            act = ffn_activation(up, ffn_conv_w[l], ffn_conv_b[l])
            h = matmul_residual(act, ffn_w_down[l].astype(BF16), h)
        outs.append(final_norm(h, final_norm_g))
    return jnp.stack(outs, axis=0)
```

```python
import functools

import jax
import jax.numpy as jnp
from jax import lax
from jax.experimental import pallas as pl
from jax.experimental.pallas import tpu as pltpu

F32 = jnp.float32
BF16 = jnp.bfloat16

D_MODEL = 2048
HG_HEADS = 16
HG_KDIM = 128
HG_VDIM = 128
D_HG = HG_HEADS * HG_KDIM
M2_HEADS = 32
M2_HEADDIM = 64
D_M2 = M2_HEADS * M2_HEADDIM
M2_GROUPS = 8
M2_DSTATE = 128
M2_CONV = 4
M2_GROUP_W = D_M2 // M2_GROUPS
M2_HPG = M2_HEADS // M2_GROUPS
CHUNK = 64
HG_SUB = 32
XA_HEADS = 4
XA_HEADDIM = D_MODEL // XA_HEADS
D_FF = 5632
FFN_CONV = 3
NORM_EPS = 1e-6
M2_NORM_EPS = 1e-5
LB_FLOOR = 1e-30
HG_SAFE_DECAY = 80.0

OFF_HQ = 0
OFF_HF = OFF_HQ + D_HG
OFF_HI = OFF_HF + D_HG
OFF_HOG = OFF_HI + D_HG
OFF_MZ = OFF_HOG + D_HG
OFF_MX = OFF_MZ + D_M2
OFF_MB = OFF_MX + D_M2
OFF_MC = OFF_MB + M2_GROUPS * M2_DSTATE
OFF_GHG = OFF_MC + M2_GROUPS * M2_DSTATE
OFF_GM2 = OFF_GHG + D_MODEL
N_PROJ = OFF_GM2 + D_MODEL
DT_PAD = 128

HALO = 8
VMEM_LIMIT = 56 * 1024 * 1024


def _cparams(n_axes):
    return pltpu.CompilerParams(
        dimension_semantics=("arbitrary",) * n_axes,
        vmem_limit_bytes=VMEM_LIMIT)


def _rms(x, g, eps):
    return x * lax.rsqrt(jnp.mean(x * x, axis=-1, keepdims=True) + eps) * g


def _sigmoid(x):
    return 0.5 * jnp.tanh(0.5 * x) + 0.5


def _silu(x):
    hx = 0.5 * x
    return hx * jnp.tanh(hx) + hx


def _softplus(x):
    return jnp.maximum(x, 0.0) + jnp.log1p(jnp.exp(-jnp.abs(x)))


def _dot(a, b):
    return jnp.dot(a, b, preferred_element_type=F32)


def _dot_nt(a, b):
    return lax.dot_general(a, b, (((1,), (1,)), ((), ())), preferred_element_type=F32)


def _dot_tn(a, b):
    return lax.dot_general(a, b, (((0,), (0,)), ((), ())), preferred_element_type=F32)


def _split3(x):
    hi = x.astype(BF16)
    r1 = x - hi.astype(F32)
    mid = r1.astype(BF16)
    lo = (r1 - mid.astype(F32)).astype(BF16)
    return hi, mid, lo


def _seg_cumsum(x, seg):
    row = lax.broadcasted_iota(jnp.int32, x.shape, 0) % seg
    k = 1
    while k < seg:
        x = x + jnp.where(row >= k, pltpu.roll(x, k, axis=0), 0.0)
        k *= 2
    return x


def _norm_matmul_kernel(a_ref, g_ref, b_ref, o_ref, an_ref):
    @pl.when(pl.program_id(1) == 0)
    def _():
        an_ref[...] = _rms(a_ref[...], g_ref[...], NORM_EPS).astype(BF16)

    o_ref[...] = _dot(an_ref[...], b_ref[...]).astype(o_ref.dtype)


def norm_matmul(a, g, b, *, out_dtype, tm=1024, tn=1024):
    m, k = a.shape
    n = b.shape[1]
    tm = min(tm, m)
    tn = min(tn, n)
    return pl.pallas_call(
        _norm_matmul_kernel,
        grid=(m // tm, n // tn),
        in_specs=[
            pl.BlockSpec((tm, k), lambda i, j: (i, 0)),
            pl.BlockSpec((1, k), lambda i, j: (0, 0)),
            pl.BlockSpec((k, tn), lambda i, j: (0, j)),
        ],
        out_specs=pl.BlockSpec((tm, tn), lambda i, j: (i, j)),
        out_shape=jax.ShapeDtypeStruct((m, n), out_dtype),
        scratch_shapes=[pltpu.VMEM((tm, k), BF16)],
        compiler_params=_cparams(2),
        name="norm_matmul",
    )(a, g.reshape(1, k), b)


def _matmul_res_kernel(a_ref, b_ref, r_ref, o_ref):
    o_ref[...] = r_ref[...] + _dot(a_ref[...], b_ref[...])


def matmul_residual(a, b, r, *, tm=512, tn=512):
    m, k = a.shape
    n = b.shape[1]
    tm = min(tm, m)
    tn = min(tn, n)
    return pl.pallas_call(
        _matmul_res_kernel,
        grid=(m // tm, n // tn),
        in_specs=[
            pl.BlockSpec((tm, k), lambda i, j: (i, 0)),
            pl.BlockSpec((k, tn), lambda i, j: (0, j)),
            pl.BlockSpec((tm, tn), lambda i, j: (i, j)),
        ],
        out_specs=pl.BlockSpec((tm, tn), lambda i, j: (i, j)),
        out_shape=jax.ShapeDtypeStruct((m, n), F32),
        compiler_params=_cparams(2),
        name="matmul_residual",
    )(a, b, r)


def _merge_kernel(a1_ref, b1_ref, a2_ref, b2_ref, g1_ref, g2_ref, o_ref):
    y1 = _dot(a1_ref[...], b1_ref[...])
    y2 = _dot(a2_ref[...], b2_ref[...])
    o_ref[...] = (_sigmoid(g1_ref[...]) * y1 + _sigmoid(g2_ref[...]) * y2).astype(o_ref.dtype)


def merge_branches(o_hg, w_hg, o_m2, w_m2, proj, *, tm=512, tn=512):
    m, k = o_hg.shape
    n = w_hg.shape[1]
    tm = min(tm, m)
    g1 = OFF_GHG // tn
    g2 = OFF_GM2 // tn
    return pl.pallas_call(
        _merge_kernel,
        grid=(m // tm, n // tn),
        in_specs=[
            pl.BlockSpec((tm, k), lambda i, j: (i, 0)),
            pl.BlockSpec((k, tn), lambda i, j: (0, j)),
            pl.BlockSpec((tm, k), lambda i, j: (i, 0)),
            pl.BlockSpec((k, tn), lambda i, j: (0, j)),
            pl.BlockSpec((tm, tn), lambda i, j: (i, g1 + j)),
            pl.BlockSpec((tm, tn), lambda i, j: (i, g2 + j)),
        ],
        out_specs=pl.BlockSpec((tm, tn), lambda i, j: (i, j)),
        out_shape=jax.ShapeDtypeStruct((m, n), BF16),
        compiler_params=_cparams(2),
        name="merge_branches",
    )(o_hg, w_hg, o_m2, w_m2, proj, proj)


def _hgrn2_kernel(q_ref, f_ref, i_ref, og_ref, lbl_ref, ng_ref, o_ref,
                  st_ref, sc_ref, stb_ref, bl_s, q_s, k_s, *, layer, tb):
    n_sub = tb // HG_SUB

    @pl.when(pl.program_id(1) == 0)
    def _():
        st_ref[...] = jnp.zeros_like(st_ref)

    logits = lbl_ref[...]
    e = jnp.exp(logits - jnp.max(logits, axis=0, keepdims=True))
    p = e / jnp.sum(e, axis=0, keepdims=True)
    lb = jnp.zeros((1, HG_KDIM), F32)
    for i in range(1, layer + 1):
        lb = lb + p[i:i + 1, :]
    lbf = jnp.maximum(lb, LB_FLOOR)

    z = f_ref[...]
    t = jnp.exp(-jnp.abs(z))
    r = 1.0 / (1.0 + t)
    pos = z >= 0.0
    log_f = jnp.log(jnp.where(pos, 1.0 + lbf * t, t + lbf) * r)
    kk = (1.0 - lb) * jnp.where(pos, t, 1.0) * r
    q = q_ref[...] * (HG_KDIM ** -0.5)
    v = i_ref[...].astype(BF16)

    bl = _seg_cumsum(log_f, HG_SUB)
    qt = (q * jnp.exp(bl)).astype(BF16)
    kinc = (kk * jnp.exp(-bl)).astype(BF16)
    worst = jnp.max(-bl)

    def rows(a, n):
        return a[n * HG_SUB:(n + 1) * HG_SUB, :]

    for n in range(n_sub):
        sc_ref[n] = _dot_nt(rows(qt, n), rows(kinc, n))

    st = st_ref[...]
    for n in range(n_sub):
        blc = rows(bl, n)
        b_last = blc[HG_SUB - 1:HG_SUB, :]
        kdec = (rows(kk, n) * jnp.exp(b_last - blc)).astype(BF16)
        stb_ref[n] = st.astype(BF16)
        st = st * jnp.exp(b_last) + _dot_tn(rows(v, n), kdec)
    st_ref[...] = st

    col_id = lax.broadcasted_iota(jnp.int32, (HG_SUB, HG_SUB), 1)
    tri = lax.broadcasted_iota(jnp.int32, (HG_SUB, HG_SUB), 0) >= col_id

    @pl.when(worst > HG_SAFE_DECAY)
    def _():
        bl_s[...] = bl
        q_s[...] = q
        k_s[...] = kk

        def body(n, carry):
            off = pl.multiple_of(n * HG_SUB, HG_SUB)
            blc = bl_s[pl.ds(off, HG_SUB), :]
            qc = q_s[pl.ds(off, HG_SUB), :]
            kc = k_s[pl.ds(off, HG_SUB), :]
            sc = jnp.zeros((HG_SUB, HG_SUB), F32)
            for s in range(HG_SUB):
                d = jnp.exp(jnp.minimum(blc - blc[s:s + 1, :], 0.0))
                col = jnp.sum(qc * d * kc[s:s + 1, :], axis=-1, keepdims=True)
                sc = jnp.where(col_id == s, col, sc)
            sc_ref[n] = sc
            return carry

        lax.fori_loop(0, n_sub, body, 0)

    outs = []
    for n in range(n_sub):
        sc = jnp.where(tri, sc_ref[n], 0.0).astype(BF16)
        outs.append(_dot(sc, rows(v, n)) + _dot_nt(rows(qt, n), stb_ref[n]))
    o = jnp.concatenate(outs, axis=0) * _sigmoid(og_ref[...])
    o_ref[...] = _rms(o, ng_ref[...], NORM_EPS).astype(o_ref.dtype)


def hgrn2_branch(proj, lb_logits, norm_g, layer, *, tb=512):
    s = proj.shape[0]
    tb = min(tb, s)
    depth = lb_logits.shape[0]
    n_sub = tb // HG_SUB
    kern = functools.partial(_hgrn2_kernel, layer=layer, tb=tb)

    def col(off):
        base = off // HG_KDIM
        return pl.BlockSpec((tb, HG_KDIM), lambda h, i: (i, base + h))

    return pl.pallas_call(
        kern,
        grid=(HG_HEADS, s // tb),
        in_specs=[
            col(OFF_HQ), col(OFF_HF), col(OFF_HI), col(OFF_HOG),
            pl.BlockSpec((depth, HG_KDIM), lambda h, i: (0, h)),
            pl.BlockSpec((1, HG_VDIM), lambda h, i: (0, h)),
        ],
        out_specs=pl.BlockSpec((tb, HG_VDIM), lambda h, i: (i, h)),
        out_shape=jax.ShapeDtypeStruct((s, D_HG), BF16),
        scratch_shapes=[pltpu.VMEM((HG_VDIM, HG_KDIM), F32),
                        pltpu.VMEM((n_sub, HG_SUB, HG_SUB), F32),
                        pltpu.VMEM((n_sub, HG_VDIM, HG_KDIM), BF16),
                        pltpu.VMEM((tb, HG_KDIM), F32),
                        pltpu.VMEM((tb, HG_KDIM), F32),
                        pltpu.VMEM((tb, HG_KDIM), F32)],
        compiler_params=_cparams(2),
        name="hgrn2_branch",
    )(proj, proj, proj, proj, lb_logits, norm_g.reshape(1, D_HG))


def _dt_kernel(a_ref, g_ref, w_ref, bias_ref, alog_ref, dt_ref, cum_ref):
    u = _rms(a_ref[...], g_ref[...], NORM_EPS).astype(BF16)
    dt = _softplus(_dot(u, w_ref[...]) + bias_ref[...])
    dt_ref[...] = dt
    cum_ref[...] = _seg_cumsum(dt * (-jnp.exp(alog_ref[...])), CHUNK)


def dt_path(a, g, w_dt, dt_bias, a_log, *, tm=1024):
    m, k = a.shape
    tm = min(tm, m)

    def pad(p):
        return jnp.pad(p, (0, DT_PAD - M2_HEADS)).reshape(1, DT_PAD)

    out = jax.ShapeDtypeStruct((m, DT_PAD), F32)
    return pl.pallas_call(
        _dt_kernel,
        grid=(m // tm,),
        in_specs=[
            pl.BlockSpec((tm, k), lambda i: (i, 0)),
            pl.BlockSpec((1, k), lambda i: (0, 0)),
            pl.BlockSpec((k, DT_PAD), lambda i: (0, 0)),
            pl.BlockSpec((1, DT_PAD), lambda i: (0, 0)),
            pl.BlockSpec((1, DT_PAD), lambda i: (0, 0)),
        ],
        out_specs=[pl.BlockSpec((tm, DT_PAD), lambda i: (i, 0))] * 2,
        out_shape=[out, out],
        compiler_params=_cparams(1),
        name="dt_path",
    )(a, g.reshape(1, k), w_dt, pad(dt_bias), pad(a_log))


def _mamba2_kernel(z_ref, x_ref, b_ref, c_ref, dt_ref, cum_ref, wx_ref, wb_ref, wc_ref,
                   bx_ref, bb_ref, bc_ref, dsk_ref, ng_ref,
                   o_ref, ext_ref, st_ref, stb_ref, y_ref, *, tb):
    g = pl.program_id(0)
    gw = M2_GROUP_W
    ns = M2_DSTATE
    w_all = gw + 2 * ns
    n_chunk = tb // CHUNK

    @pl.when(pl.program_id(1) == 0)
    def _():
        st_ref[...] = jnp.zeros_like(st_ref)
        ext_ref[0:HALO, :] = jnp.zeros((HALO, w_all), F32)

    ext_ref[HALO:HALO + tb, 0:gw] = x_ref[...]
    ext_ref[HALO:HALO + tb, gw:gw + ns] = b_ref[...]
    ext_ref[HALO:HALO + tb, gw + ns:w_all] = c_ref[...]
    w = jnp.concatenate([wx_ref[...], wb_ref[...], wc_ref[...]], axis=1)
    bias = jnp.concatenate([bx_ref[...], bb_ref[...], bc_ref[...]], axis=1)
    acc = bias + w[M2_CONV - 1:M2_CONV, :] * ext_ref[HALO:HALO + tb, :]
    for j in range(1, M2_CONV):
        acc = acc + w[M2_CONV - 1 - j:M2_CONV - j, :] * ext_ref[HALO - j:HALO - j + tb, :]
    ext_ref[0:HALO, :] = ext_ref[tb:tb + HALO, :]
    xbc = _silu(acc)
    xs = xbc[:, 0:gw]
    bm = xbc[:, gw:gw + ns].astype(BF16)
    cm = xbc[:, gw + ns:w_all].astype(BF16)

    r_id = lax.broadcasted_iota(jnp.int32, (DT_PAD, gw), 0)
    c_id = lax.broadcasted_iota(jnp.int32, (DT_PAD, gw), 1)
    expand = (r_id == g * M2_HPG + c_id // M2_HEADDIM).astype(BF16)
    d_parts = _split3(dt_ref[...])
    c_parts = _split3(cum_ref[...])
    dtx = _dot(d_parts[0], expand) + _dot(d_parts[1], expand) + _dot(d_parts[2], expand)
    cumx = _dot(c_parts[0], expand) + _dot(c_parts[1], expand) + _dot(c_parts[2], expand)
    xdt = xs * dtx
    ecum = jnp.exp(cumx)

    sr = lax.broadcasted_iota(jnp.int32, (HALO, DT_PAD), 0)
    sl = lax.broadcasted_iota(jnp.int32, (HALO, DT_PAD), 1)
    sel = jnp.logical_and(sr < M2_HPG, sl == g * M2_HPG + sr).astype(BF16)

    def rot(a):
        return jnp.concatenate([a[tb - CHUNK:, :], a[:tb - CHUNK, :]], axis=0)

    rows_all = _dot_nt(sel, c_parts[0]) + _dot_nt(sel, c_parts[1]) + _dot_nt(sel, c_parts[2])
    rows_rot = (_dot_nt(sel, rot(c_parts[0])) + _dot_nt(sel, rot(c_parts[1]))
                + _dot_nt(sel, rot(c_parts[2])))

    pair_w = 2 * CHUNK
    low_half = lax.broadcasted_iota(jnp.int32, (1, pair_w), 1) < CHUNK
    tri = (lax.broadcasted_iota(jnp.int32, (CHUNK, pair_w), 0)
           >= lax.broadcasted_iota(jnp.int32, (CHUNK, pair_w), 1) % CHUNK)
    head_of_col = lax.broadcasted_iota(jnp.int32, (1, gw), 1) // M2_HEADDIM

    def rows(a, n):
        return a[n * CHUNK:(n + 1) * CHUNK, :]

    for n in range(n_chunk):
        cb4 = _dot_nt(rows(cm, n), jnp.concatenate([rows(bm, n)] * M2_HPG, axis=0))
        cumc = rows(cumx, n)
        lparts = []
        for p_ in range(M2_HPG // 2):
            if n % 2 == 0:
                a0 = n * CHUNK
                rp = jnp.where(low_half, rows_all[2 * p_:2 * p_ + 1, a0:a0 + pair_w],
                               rows_rot[2 * p_ + 1:2 * p_ + 2, a0:a0 + pair_w])
            else:
                a_lo = ((n + 1) * CHUNK) % tb
                a_hi = (n - 1) * CHUNK
                rp = jnp.where(low_half, rows_rot[2 * p_:2 * p_ + 1, a_lo:a_lo + pair_w],
                               rows_all[2 * p_ + 1:2 * p_ + 2, a_hi:a_hi + pair_w])
            diff = cumc[:, p_ * pair_w:(p_ + 1) * pair_w] - rp
            lparts.append(jnp.where(tri, jnp.exp(jnp.where(tri, diff, 0.0)), 0.0))
        lhs = (cb4 * jnp.concatenate(lparts, axis=1)).astype(BF16)
        xdtb = rows(xdt, n).astype(BF16)
        rhs = jnp.concatenate(
            [jnp.where(head_of_col == h, xdtb, jnp.zeros_like(xdtb)) for h in range(M2_HPG)], axis=0)
        y_ref[n * CHUNK:(n + 1) * CHUNK, :] = _dot(lhs, rhs)

    st = st_ref[...]
    for n in range(n_chunk):
        cumc = rows(cumx, n)
        last = cumc[CHUNK - 1:CHUNK, :]
        wdec = (rows(xdt, n) * jnp.exp(last - cumc)).astype(BF16)
        stb_ref[n] = st.astype(BF16)
        st = st * jnp.exp(last) + _dot_tn(rows(bm, n), wdec)
    st_ref[...] = st

    for n in range(n_chunk):
        y_ref[n * CHUNK:(n + 1) * CHUNK, :] += _dot(rows(cm, n), stb_ref[n]) * rows(ecum, n)

    y = (y_ref[...] + dsk_ref[...] * xs) * _silu(z_ref[...])
    o_ref[...] = _rms(y, ng_ref[...], M2_NORM_EPS).astype(o_ref.dtype)


def mamba2_branch(proj, dt, cum, conv_w, conv_b, d_x, norm_g, *, tb=512):
    s = proj.shape[0]
    tb = min(tb, s)
    kern = functools.partial(_mamba2_kernel, tb=tb)
    gw = M2_GROUP_W
    ns = M2_DSTATE
    w_all = gw + 2 * ns

    def colw(off, width):
        base = off // width
        return pl.BlockSpec((tb, width), lambda g, i: (i, base + g))

    def par(rows, off, width):
        base = off // width
        return pl.BlockSpec((rows, width), lambda g, i: (0, base + g))

    conv_b = conv_b.reshape(1, -1)
    return pl.pallas_call(
        kern,
        grid=(M2_GROUPS, s // tb),
        in_specs=[
            colw(OFF_MZ, gw), colw(OFF_MX, gw), colw(OFF_MB, ns), colw(OFF_MC, ns),
            pl.BlockSpec((tb, DT_PAD), lambda g, i: (i, 0)),
            pl.BlockSpec((tb, DT_PAD), lambda g, i: (i, 0)),
            par(M2_CONV, 0, gw), par(M2_CONV, D_M2, ns), par(M2_CONV, D_M2 + M2_GROUPS * ns, ns),
            par(1, 0, gw), par(1, D_M2, ns), par(1, D_M2 + M2_GROUPS * ns, ns),
            par(1, 0, gw), par(1, 0, gw),
        ],
        out_specs=pl.BlockSpec((tb, gw), lambda g, i: (i, g)),
        out_shape=jax.ShapeDtypeStruct((s, D_M2), BF16),
        scratch_shapes=[pltpu.VMEM((tb + HALO, w_all), F32),
                        pltpu.VMEM((ns, gw), F32),
                        pltpu.VMEM((tb // CHUNK, ns, gw), BF16),
                        pltpu.VMEM((tb, gw), F32)],
        compiler_params=_cparams(2),
        name="mamba2_branch",
    )(proj, proj, proj, proj, dt, cum, conv_w, conv_w, conv_w, conv_b, conv_b, conv_b,
      d_x, norm_g.reshape(1, D_M2))


def _xattn_kernel(q_ref, k_ref, v_ref, o_ref):
    scale = XA_HEADDIM ** -0.5
    for h in range(XA_HEADS):
        lo = h * XA_HEADDIM
        hi = lo + XA_HEADDIM
        sc = _dot_nt(q_ref[:, lo:hi], k_ref[:, lo:hi]) * scale
        e = jnp.exp(sc - jnp.max(sc, axis=-1, keepdims=True))
        p = e / jnp.sum(e, axis=-1, keepdims=True)
        o_ref[:, lo:hi] = _dot(p.astype(BF16), v_ref[:, lo:hi]).astype(o_ref.dtype)


def cross_attention(q, kv, *, tm=512):
    s = q.shape[0]
    n_mem = kv.shape[0]
    tm = min(tm, s)
    return pl.pallas_call(
        _xattn_kernel,
        grid=(s // tm,),
        in_specs=[
            pl.BlockSpec((tm, D_MODEL), lambda i: (i, 0)),
            pl.BlockSpec((n_mem, D_MODEL), lambda i: (0, 0)),
            pl.BlockSpec((n_mem, D_MODEL), lambda i: (0, 1)),
        ],
        out_specs=pl.BlockSpec((tm, D_MODEL), lambda i: (i, 0)),
        out_shape=jax.ShapeDtypeStruct((s, D_MODEL), BF16),
        compiler_params=_cparams(1),
        name="cross_attention",
    )(q, kv, kv)


def _ffn_act_kernel(gate_ref, up_ref, w_ref, b_ref, o_ref, ext_ref, *, tb):
    @pl.when(pl.program_id(1) == 0)
    def _():
        ext_ref[0:HALO, :] = jnp.zeros((HALO, ext_ref.shape[1]), F32)

    ext_ref[HALO:HALO + tb, :] = gate_ref[...]
    w = w_ref[...]
    acc = b_ref[...] + w[FFN_CONV - 1:FFN_CONV, :] * ext_ref[HALO:HALO + tb, :]
    for j in range(1, FFN_CONV):
        acc = acc + w[FFN_CONV - 1 - j:FFN_CONV - j, :] * ext_ref[HALO - j:HALO - j + tb, :]
    ext_ref[0:HALO, :] = ext_ref[tb:tb + HALO, :]
    gelu = 0.5 * acc * (1.0 + lax.erf(acc * (2.0 ** -0.5)))
    o_ref[...] = (gelu * up_ref[...]).astype(o_ref.dtype)


def ffn_activation(up, conv_w, conv_b, *, tb=512, tn=512):
    s = up.shape[0]
    tb = min(tb, s)
    n_col = D_FF // tn
    kern = functools.partial(_ffn_act_kernel, tb=tb)
    return pl.pallas_call(
        kern,
        grid=(n_col, s // tb),
        in_specs=[
            pl.BlockSpec((tb, tn), lambda j, i: (i, j)),
            pl.BlockSpec((tb, tn), lambda j, i: (i, n_col + j)),
            pl.BlockSpec((FFN_CONV, tn), lambda j, i: (0, j)),
            pl.BlockSpec((1, tn), lambda j, i: (0, j)),
        ],
        out_specs=pl.BlockSpec((tb, tn), lambda j, i: (i, j)),
        out_shape=jax.ShapeDtypeStruct((s, D_FF), BF16),
        scratch_shapes=[pltpu.VMEM((tb + HALO, tn), F32)],
        compiler_params=_cparams(2),
        name="ffn_activation",
    )(up, up, conv_w, conv_b.reshape(1, D_FF))


def _final_norm_kernel(x_ref, g_ref, o_ref):
    o_ref[...] = _rms(x_ref[...], g_ref[...], NORM_EPS)


def final_norm(x, g, *, tm=512):
    m, k = x.shape
    tm = min(tm, m)
    return pl.pallas_call(
        _final_norm_kernel,
        grid=(m // tm,),
        in_specs=[pl.BlockSpec((tm, k), lambda i: (i, 0)),
                  pl.BlockSpec((1, k), lambda i: (0, 0))],
        out_specs=pl.BlockSpec((tm, k), lambda i: (i, 0)),
        out_shape=jax.ShapeDtypeStruct((m, k), F32),
        compiler_params=_cparams(1),
        name="final_norm",
    )(x, g.reshape(1, k))


def _repack_w_in(w):
    dt0 = OFF_MC + M2_GROUPS * M2_DSTATE
    main = jnp.concatenate([w[:, :dt0], w[:, dt0 + M2_HEADS:]], axis=1).astype(BF16)
    w_dt = jnp.pad(w[:, dt0:dt0 + M2_HEADS], ((0, 0), (0, DT_PAD - M2_HEADS))).astype(BF16)
    return main, w_dt


def _per_head_lanes(p):
    return jnp.repeat(p, M2_HEADDIM).reshape(1, D_M2)


def kernel(x, mem, mix_norm_g, w_in, hg_lb_logits, hg_norm_g, m2_conv_w, m2_conv_b, m2_dt_bias, m2_A_log, m2_D, m2_norm_g, w_branch_hg, w_branch_m2, w_out, mem_norm_g, xa_norm_g, xa_wq, xa_wkv, xa_wo, ffn_norm_g, ffn_w_up, ffn_conv_w, ffn_conv_b, ffn_w_down, final_norm_g):
    bsz, seq, d = x.shape
    depth = w_in.shape[0]
    outs = []
    for b in range(bsz):
        h = x[b]
        mem_b = mem[b]
        for l in range(depth):
            w_main, w_dt = _repack_w_in(w_in[l])
            proj = norm_matmul(h, mix_norm_g[l], w_main, out_dtype=F32)
            dt, cum = dt_path(h, mix_norm_g[l], w_dt, m2_dt_bias[l], m2_A_log[l])
            o_hg = hgrn2_branch(proj, hg_lb_logits, hg_norm_g[l], l)
            o_m2 = mamba2_branch(proj, dt, cum, m2_conv_w[l], m2_conv_b[l],
                                 _per_head_lanes(m2_D[l]), m2_norm_g[l])
            merged = merge_branches(o_hg, w_branch_hg[l].astype(BF16),
                                    o_m2, w_branch_m2[l].astype(BF16), proj)
            h = matmul_residual(merged, w_out[l].astype(BF16), h)

            q = norm_matmul(h, xa_norm_g[l], xa_wq[l].astype(BF16), out_dtype=BF16)
            kv = norm_matmul(mem_b, mem_norm_g, xa_wkv[l].astype(BF16), out_dtype=BF16)
            att = cross_attention(q, kv)
            h = matmul_residual(att, xa_wo[l].astype(BF16), h)

            up = norm_matmul(h, ffn_norm_g[l], ffn_w_up[l].astype(BF16), out_dtype=F32)
            act = ffn_activation(up, ffn_conv_w[l], ffn_conv_b[l])
            h = matmul_residual(act, ffn_w_down[l].astype(BF16), h)
        outs.append(final_norm(h, final_norm_g))
    return jnp.stack(outs, axis=0)
```

```python
import functools

import jax
import jax.numpy as jnp
from jax import lax
from jax.experimental import pallas as pl
from jax.experimental.pallas import tpu as pltpu

F32 = jnp.float32
BF16 = jnp.bfloat16

D_MODEL = 2048
HG_HEADS = 16
HG_KDIM = 128
HG_VDIM = 128
D_HG = HG_HEADS * HG_KDIM
M2_HEADS = 32
M2_HEADDIM = 64
D_M2 = M2_HEADS * M2_HEADDIM
M2_GROUPS = 8
M2_DSTATE = 128
M2_CONV = 4
M2_GROUP_W = D_M2 // M2_GROUPS
M2_HPG = M2_HEADS // M2_GROUPS
D_BC = M2_GROUPS * M2_DSTATE
CHUNK = 64
HG_SUB = 32
HG_HPS = 4
XA_HEADS = 4
XA_HEADDIM = D_MODEL // XA_HEADS
D_FF = 5632
FFN_CONV = 3
NORM_EPS = 1e-6
M2_NORM_EPS = 1e-5
LB_FLOOR = 1e-30
HG_SAFE_DECAY = 80.0

IN_MZ = 4 * D_HG
IN_DT = IN_MZ + 2 * D_M2 + 2 * D_BC
IN_GHG = IN_DT + M2_HEADS
IN_GM2 = IN_GHG + D_MODEL
OFF_HQ = 0
OFF_HF = OFF_HQ + D_HG
OFF_HI = OFF_HF + D_HG
OFF_HOG = OFF_HI + D_HG
OFF_GM2 = OFF_HOG + D_HG
DU_MZ = 0
DU_X = DU_MZ + D_M2
DU_BC = DU_X + D_M2
DU_GHG = DU_BC + 2 * D_BC
DT_PAD = 128

HALO = 8
VMEM_LIMIT = 56 * 1024 * 1024


def _cparams(n_axes):
    return pltpu.CompilerParams(
        dimension_semantics=("arbitrary",) * n_axes,
        vmem_limit_bytes=VMEM_LIMIT)


def _rms(x, g, eps):
    return x * lax.rsqrt(jnp.mean(x * x, axis=-1, keepdims=True) + eps) * g


def _sigmoid(x):
    return 0.5 * jnp.tanh(0.5 * x) + 0.5


def _silu(x):
    hx = 0.5 * x
    return hx * jnp.tanh(hx) + hx


def _softplus(x):
    return jnp.maximum(x, 0.0) + jnp.log1p(jnp.exp(-jnp.abs(x)))


def _dot(a, b):
    return jnp.dot(a, b, preferred_element_type=F32)


def _dot_nt(a, b):
    return lax.dot_general(a, b, (((1,), (1,)), ((), ())), preferred_element_type=F32)


def _dot_tn(a, b):
    return lax.dot_general(a, b, (((0,), (0,)), ((), ())), preferred_element_type=F32)


def _split3(x):
    hi = x.astype(BF16)
    r1 = x - hi.astype(F32)
    mid = r1.astype(BF16)
    lo = (r1 - mid.astype(F32)).astype(BF16)
    return hi, mid, lo


def _seg_cumsum(x, seg):
    row = lax.broadcasted_iota(jnp.int32, x.shape, 0) % seg
    k = 1
    while k < seg:
        x = x + jnp.where(row >= k, pltpu.roll(x, k, axis=0), 0.0)
        k *= 2
    return x


def _wspec(k, tn, layer, col0=0):
    return pl.BlockSpec((None, k, tn), lambda i, j: (layer, 0, col0 + j))


def _norm_matmul_kernel(a_ref, g_ref, b_ref, o_ref, an_ref):
    @pl.when(pl.program_id(1) == 0)
    def _():
        an_ref[...] = _rms(a_ref[...], g_ref[...], NORM_EPS).astype(BF16)

    o_ref[...] = _dot(an_ref[...], b_ref[...]).astype(o_ref.dtype)


def norm_matmul(a, g, w, layer, *, out_dtype, tm=1024, tn=1024):
    m, k = a.shape
    n = w.shape[2]
    tm = min(tm, m)
    tn = min(tn, n)
    return pl.pallas_call(
        _norm_matmul_kernel,
        grid=(m // tm, n // tn),
        in_specs=[
            pl.BlockSpec((tm, k), lambda i, j: (i, 0)),
            pl.BlockSpec((1, k), lambda i, j: (0, 0)),
            _wspec(k, tn, layer),
        ],
        out_specs=pl.BlockSpec((tm, tn), lambda i, j: (i, j)),
        out_shape=jax.ShapeDtypeStruct((m, n), out_dtype),
        scratch_shapes=[pltpu.VMEM((tm, k), BF16)],
        compiler_params=_cparams(2),
        name="norm_matmul",
    )(a, g.reshape(1, k), w)


def _norm_matmul_conv_kernel(a_ref, g_ref, *refs, kw, glu, tm):
    if glu:
        wg_ref, wu_ref, cw_ref, cb_ref, o_ref, an_ref, halo_ref, ext_ref = refs
    else:
        wg_ref, wu_ref, cw_ref, cb_ref, o_ref, o2_ref, an_ref, halo_ref, ext_ref = refs
    i = pl.program_id(0)
    j = pl.program_id(1)

    @pl.when(jnp.logical_and(i == 0, j == 0))
    def _():
        halo_ref[...] = jnp.zeros_like(halo_ref)

    @pl.when(j == 0)
    def _():
        an_ref[...] = _rms(a_ref[...], g_ref[...], NORM_EPS).astype(BF16)

    an = an_ref[...]
    ext_ref[0:HALO, :] = halo_ref[j]
    ext_ref[HALO:HALO + tm, :] = _dot(an, wg_ref[...])
    w = cw_ref[...]
    acc = cb_ref[...] + w[kw - 1:kw, :] * ext_ref[HALO:HALO + tm, :]
    for t in range(1, kw):
        acc = acc + w[kw - 1 - t:kw - t, :] * ext_ref[HALO - t:HALO - t + tm, :]
    halo_ref[j] = ext_ref[tm:tm + HALO, :]
    if glu:
        gelu = 0.5 * acc * (1.0 + lax.erf(acc * (2.0 ** -0.5)))
        out = gelu * _dot(an, wu_ref[...])
    else:
        out = _silu(acc)
        o2_ref[...] = _dot(an, wu_ref[...])
    o_ref[...] = out.astype(o_ref.dtype)


def norm_matmul_conv(a, g, w, layer, conv_w, conv_b, *, n_out, w_col0, w2_col0, glu,
                     conv_col0=0, out_dtype, tm=1024, tn=512):
    m, k = a.shape
    kw = conv_w.shape[0]
    tm = min(tm, m)
    n_col = n_out // tn
    kern = functools.partial(_norm_matmul_conv_kernel, kw=kw, glu=glu, tm=tm)
    cc = conv_col0 // tn
    o_spec = pl.BlockSpec((tm, tn), lambda i, j: (i, j))
    o_shape = jax.ShapeDtypeStruct((m, n_out), out_dtype)
    return pl.pallas_call(
        kern,
        grid=(m // tm, n_col),
        in_specs=[
            pl.BlockSpec((tm, k), lambda i, j: (i, 0)),
            pl.BlockSpec((1, k), lambda i, j: (0, 0)),
            _wspec(k, tn, layer, w_col0 // tn),
            _wspec(k, tn, layer, w2_col0 // tn),
            pl.BlockSpec((kw, tn), lambda i, j: (0, cc + j)),
            pl.BlockSpec((1, tn), lambda i, j: (0, cc + j)),
        ],
        out_specs=o_spec if glu else [o_spec, o_spec],
        out_shape=o_shape if glu else [o_shape, jax.ShapeDtypeStruct((m, n_out), F32)],
        scratch_shapes=[pltpu.VMEM((tm, k), BF16),
                        pltpu.VMEM((n_col, HALO, tn), F32),
                        pltpu.VMEM((tm + HALO, tn), F32)],
        compiler_params=_cparams(2),
        name="norm_matmul_conv_glu" if glu else "norm_matmul_conv",
    )(a, g.reshape(1, k), w, w, conv_w, conv_b.reshape(1, -1))


def _matmul_res_kernel(a_ref, b_ref, r_ref, o_ref):
    o_ref[...] = r_ref[...] + _dot(a_ref[...], b_ref[...])


def matmul_residual(a, w, layer, r, *, tm, tn):
    m, k = a.shape
    n = w.shape[2]
    tm = min(tm, m)
    tn = min(tn, n)
    return pl.pallas_call(
        _matmul_res_kernel,
        grid=(m // tm, n // tn),
        in_specs=[
            pl.BlockSpec((tm, k), lambda i, j: (i, 0)),
            _wspec(k, tn, layer),
            pl.BlockSpec((tm, tn), lambda i, j: (i, j)),
        ],
        out_specs=pl.BlockSpec((tm, tn), lambda i, j: (i, j)),
        out_shape=jax.ShapeDtypeStruct((m, n), F32),
        compiler_params=_cparams(2),
        name="matmul_residual",
    )(a, w, r)


def _merge_kernel(a1_ref, b1_ref, a2_ref, b2_ref, g1_ref, g2_ref, o_ref):
    y1 = _dot(a1_ref[...], b1_ref[...])
    y2 = _dot(a2_ref[...], b2_ref[...])
    o_ref[...] = (_sigmoid(g1_ref[...]) * y1 + _sigmoid(g2_ref[...]) * y2).astype(o_ref.dtype)


def merge_branches(o_hg, w_hg, o_m2, w_m2, layer, g_hg, proj, *, tm=512, tn=1024):
    m, k = o_hg.shape
    n = w_hg.shape[2]
    tm = min(tm, m)
    g2 = OFF_GM2 // tn
    return pl.pallas_call(
        _merge_kernel,
        grid=(m // tm, n // tn),
        in_specs=[
            pl.BlockSpec((tm, k), lambda i, j: (i, 0)),
            _wspec(k, tn, layer),
            pl.BlockSpec((tm, k), lambda i, j: (i, 0)),
            _wspec(k, tn, layer),
            pl.BlockSpec((tm, tn), lambda i, j: (i, j)),
            pl.BlockSpec((tm, tn), lambda i, j: (i, g2 + j)),
        ],
        out_specs=pl.BlockSpec((tm, tn), lambda i, j: (i, j)),
        out_shape=jax.ShapeDtypeStruct((m, n), BF16),
        compiler_params=_cparams(2),
        name="merge_branches",
    )(o_hg, w_hg, o_m2, w_m2, g_hg, proj)


def _hgrn2_kernel(q_ref, f_ref, i_ref, og_ref, lbl_ref, ng_ref, o_ref,
                  st_ref, sc_ref, stb_ref, bl_s, q_s, k_s, *, layer, tb):
    n_sub = tb // HG_SUB
    width = HG_HPS * HG_KDIM

    @pl.when(pl.program_id(1) == 0)
    def _():
        st_ref[...] = jnp.zeros_like(st_ref)

    logits = lbl_ref[...]
    e = jnp.exp(logits - jnp.max(logits, axis=0, keepdims=True))
    p = e / jnp.sum(e, axis=0, keepdims=True)
    lb = jnp.zeros((1, width), F32)
    for i in range(1, layer + 1):
        lb = lb + p[i:i + 1, :]
    lbf = jnp.maximum(lb, LB_FLOOR)

    z = f_ref[...]
    t = jnp.exp(-jnp.abs(z))
    r = 1.0 / (1.0 + t)
    pos = z >= 0.0
    log_f = jnp.log(jnp.where(pos, 1.0 + lbf * t, t + lbf) * r)
    kk = (1.0 - lb) * jnp.where(pos, t, 1.0) * r
    q = q_ref[...] * (HG_KDIM ** -0.5)
    v = i_ref[...].astype(BF16)

    bl = _seg_cumsum(log_f, HG_SUB)
    qt = (q * jnp.exp(bl)).astype(BF16)
    kinc = (kk * jnp.exp(-bl)).astype(BF16)
    worst = jnp.max(-bl)

    def blk(a, n, h):
        return a[n * HG_SUB:(n + 1) * HG_SUB, h * HG_KDIM:(h + 1) * HG_KDIM]

    for h in range(HG_HPS):
        for n in range(n_sub):
            sc_ref[h * n_sub + n] = _dot_nt(blk(qt, n, h), blk(kinc, n, h))

    for h in range(HG_HPS):
        st = st_ref[h]
        for n in range(n_sub):
            blc = blk(bl, n, h)
            b_last = blc[HG_SUB - 1:HG_SUB, :]
            kdec = (blk(kk, n, h) * jnp.exp(b_last - blc)).astype(BF16)
            stb_ref[h * n_sub + n] = st.astype(BF16)
            st = st * jnp.exp(b_last) + _dot_tn(blk(v, n, h), kdec)
        st_ref[h] = st

    col_id = lax.broadcasted_iota(jnp.int32, (HG_SUB, HG_SUB), 1)
    tri = lax.broadcasted_iota(jnp.int32, (HG_SUB, HG_SUB), 0) >= col_id

    @pl.when(worst > HG_SAFE_DECAY)
    def _():
        for h in range(HG_HPS):
            cols = slice(h * HG_KDIM, (h + 1) * HG_KDIM)
            bl_s[h * tb:(h + 1) * tb, :] = bl[:, cols]
            q_s[h * tb:(h + 1) * tb, :] = q[:, cols]
            k_s[h * tb:(h + 1) * tb, :] = kk[:, cols]

        def body(n, carry):
            off = pl.multiple_of(n * HG_SUB, HG_SUB)
            blc = bl_s[pl.ds(off, HG_SUB), :]
            qc = q_s[pl.ds(off, HG_SUB), :]
            kc = k_s[pl.ds(off, HG_SUB), :]
            sc = jnp.zeros((HG_SUB, HG_SUB), F32)
            for s in range(HG_SUB):
                d = jnp.exp(jnp.minimum(blc - blc[s:s + 1, :], 0.0))
                col = jnp.sum(qc * d * kc[s:s + 1, :], axis=-1, keepdims=True)
                sc = jnp.where(col_id == s, col, sc)
            sc_ref[n] = sc
            return carry

        lax.fori_loop(0, HG_HPS * n_sub, body, 0)

    for h in range(HG_HPS):
        cols = slice(h * HG_KDIM, (h + 1) * HG_KDIM)
        outs = []
        for n in range(n_sub):
            sc = jnp.where(tri, sc_ref[h * n_sub + n], 0.0).astype(BF16)
            outs.append(_dot(sc, blk(v, n, h)) + _dot_nt(blk(qt, n, h), stb_ref[h * n_sub + n]))
        o = jnp.concatenate(outs, axis=0) * _sigmoid(og_ref[:, cols])
        o_ref[:, cols] = _rms(o, ng_ref[:, cols], NORM_EPS).astype(o_ref.dtype)


def hgrn2_branch(proj, lb_logits, norm_g, layer, *, tb=512):
    s = proj.shape[0]
    tb = min(tb, s)
    depth = lb_logits.shape[0]
    n_sub = tb // HG_SUB
    width = HG_HPS * HG_KDIM
    kern = functools.partial(_hgrn2_kernel, layer=layer, tb=tb)

    def col(off):
        base = off // width
        return pl.BlockSpec((tb, width), lambda h, i: (i, base + h))

    return pl.pallas_call(
        kern,
        grid=(HG_HEADS // HG_HPS, s // tb),
        in_specs=[
            col(OFF_HQ), col(OFF_HF), col(OFF_HI), col(OFF_HOG),
            pl.BlockSpec((depth, width), lambda h, i: (0, h)),
            pl.BlockSpec((1, width), lambda h, i: (0, h)),
        ],
        out_specs=pl.BlockSpec((tb, width), lambda h, i: (i, h)),
        out_shape=jax.ShapeDtypeStruct((s, D_HG), BF16),
        scratch_shapes=[pltpu.VMEM((HG_HPS, HG_VDIM, HG_KDIM), F32),
                        pltpu.VMEM((HG_HPS * n_sub, HG_SUB, HG_SUB), F32),
                        pltpu.VMEM((HG_HPS * n_sub, HG_VDIM, HG_KDIM), BF16),
                        pltpu.VMEM((HG_HPS * tb, HG_KDIM), F32),
                        pltpu.VMEM((HG_HPS * tb, HG_KDIM), F32),
                        pltpu.VMEM((HG_HPS * tb, HG_KDIM), F32)],
        compiler_params=_cparams(2),
        name="hgrn2_branch",
    )(proj, proj, proj, proj, lb_logits, norm_g.reshape(1, D_HG))


def _dt_kernel(a_ref, g_ref, w_ref, bias_ref, alog_ref, dt_ref, cum_ref):
    u = _rms(a_ref[...], g_ref[...], NORM_EPS).astype(BF16)
    dt = _softplus(_dot(u, w_ref[...].astype(BF16)) + bias_ref[...])
    dt_ref[...] = dt
    cum_ref[...] = _seg_cumsum(dt * (-jnp.exp(alog_ref[...])), CHUNK)


def dt_path(a, g, w_in, layer, dt_bias, a_log, *, tm=1024):
    m, k = a.shape
    tm = min(tm, m)

    def pad(p):
        return jnp.pad(p, (0, DT_PAD - M2_HEADS)).reshape(1, DT_PAD)

    out = jax.ShapeDtypeStruct((m, DT_PAD), F32)
    return pl.pallas_call(
        _dt_kernel,
        grid=(m // tm,),
        in_specs=[
            pl.BlockSpec((tm, k), lambda i: (i, 0)),
            pl.BlockSpec((1, k), lambda i: (0, 0)),
            pl.BlockSpec((None, k, DT_PAD), lambda i: (layer, 0, IN_DT // DT_PAD)),
            pl.BlockSpec((1, DT_PAD), lambda i: (0, 0)),
            pl.BlockSpec((1, DT_PAD), lambda i: (0, 0)),
        ],
        out_specs=[pl.BlockSpec((tm, DT_PAD), lambda i: (i, 0))] * 2,
        out_shape=[out, out],
        compiler_params=_cparams(1),
        name="dt_path",
    )(a, g.reshape(1, k), w_in, pad(dt_bias), pad(a_log))


def _mamba2_kernel(z_ref, x_ref, b_ref, c_ref, dt_ref, cum_ref, dsk_ref, ng_ref,
                   o_ref, st_ref, stb_ref, y_ref, *, tb):
    g = pl.program_id(0)
    gw = M2_GROUP_W
    n_chunk = tb // CHUNK

    @pl.when(pl.program_id(1) == 0)
    def _():
        st_ref[...] = jnp.zeros_like(st_ref)

    xs = x_ref[...]
    bm = b_ref[...]
    cm = c_ref[...]

    r_id = lax.broadcasted_iota(jnp.int32, (DT_PAD, gw), 0)
    c_id = lax.broadcasted_iota(jnp.int32, (DT_PAD, gw), 1)
    expand = (r_id == g * M2_HPG + c_id // M2_HEADDIM).astype(BF16)
    d_parts = _split3(dt_ref[...])
    c_parts = _split3(cum_ref[...])
    dtx = _dot(d_parts[0], expand) + _dot(d_parts[1], expand) + _dot(d_parts[2], expand)
    cumx = _dot(c_parts[0], expand) + _dot(c_parts[1], expand) + _dot(c_parts[2], expand)
    xdt = xs * dtx
    ecum = jnp.exp(cumx)

    sr = lax.broadcasted_iota(jnp.int32, (HALO, DT_PAD), 0)
    sl = lax.broadcasted_iota(jnp.int32, (HALO, DT_PAD), 1)
    sel = jnp.logical_and(sr < M2_HPG, sl == g * M2_HPG + sr).astype(BF16)

    def rot(a):
        return jnp.concatenate([a[tb - CHUNK:, :], a[:tb - CHUNK, :]], axis=0)

    rows_all = _dot_nt(sel, c_parts[0]) + _dot_nt(sel, c_parts[1]) + _dot_nt(sel, c_parts[2])
    rows_rot = (_dot_nt(sel, rot(c_parts[0])) + _dot_nt(sel, rot(c_parts[1]))
                + _dot_nt(sel, rot(c_parts[2])))

    pair_w = 2 * CHUNK
    low_half = lax.broadcasted_iota(jnp.int32, (1, pair_w), 1) < CHUNK
    tri = (lax.broadcasted_iota(jnp.int32, (CHUNK, pair_w), 0)
           >= lax.broadcasted_iota(jnp.int32, (CHUNK, pair_w), 1) % CHUNK)
    head_of_col = lax.broadcasted_iota(jnp.int32, (1, gw), 1) // M2_HEADDIM

    def rows(a, n):
        return a[n * CHUNK:(n + 1) * CHUNK, :]

    for n in range(n_chunk):
        cb4 = _dot_nt(rows(cm, n), jnp.concatenate([rows(bm, n)] * M2_HPG, axis=0))
        cumc = rows(cumx, n)
        lparts = []
        for p_ in range(M2_HPG // 2):
            if n % 2 == 0:
                a0 = n * CHUNK
                rp = jnp.where(low_half, rows_all[2 * p_:2 * p_ + 1, a0:a0 + pair_w],
                               rows_rot[2 * p_ + 1:2 * p_ + 2, a0:a0 + pair_w])
            else:
                a_lo = ((n + 1) * CHUNK) % tb
                a_hi = (n - 1) * CHUNK
                rp = jnp.where(low_half, rows_rot[2 * p_:2 * p_ + 1, a_lo:a_lo + pair_w],
                               rows_all[2 * p_ + 1:2 * p_ + 2, a_hi:a_hi + pair_w])
            diff = cumc[:, p_ * pair_w:(p_ + 1) * pair_w] - rp
            lparts.append(jnp.where(tri, jnp.exp(jnp.where(tri, diff, 0.0)), 0.0))
        lhs = (cb4 * jnp.concatenate(lparts, axis=1)).astype(BF16)
        xdtb = rows(xdt, n).astype(BF16)
        rhs = jnp.concatenate(
            [jnp.where(head_of_col == h, xdtb, jnp.zeros_like(xdtb)) for h in range(M2_HPG)], axis=0)
        y_ref[n * CHUNK:(n + 1) * CHUNK, :] = _dot(lhs, rhs)

    st = st_ref[...]
    for n in range(n_chunk):
        cumc = rows(cumx, n)
        last = cumc[CHUNK - 1:CHUNK, :]
        wdec = (rows(xdt, n) * jnp.exp(last - cumc)).astype(BF16)
        stb_ref[n] = st.astype(BF16)
        st = st * jnp.exp(last) + _dot_tn(rows(bm, n), wdec)
    st_ref[...] = st

    for n in range(n_chunk):
        y_ref[n * CHUNK:(n + 1) * CHUNK, :] += _dot(rows(cm, n), stb_ref[n]) * rows(ecum, n)

    y = (y_ref[...] + dsk_ref[...] * xs) * _silu(z_ref[...])
    o_ref[...] = _rms(y, ng_ref[...], M2_NORM_EPS).astype(o_ref.dtype)


def mamba2_branch(z, xs, bc, dt, cum, d_x, norm_g, *, tb=512):
    s = z.shape[0]
    tb = min(tb, s)
    kern = functools.partial(_mamba2_kernel, tb=tb)
    gw = M2_GROUP_W
    ns = M2_DSTATE
    return pl.pallas_call(
        kern,
        grid=(M2_GROUPS, s // tb),
        in_specs=[
            pl.BlockSpec((tb, gw), lambda g, i: (i, g)),
            pl.BlockSpec((tb, gw), lambda g, i: (i, g)),
            pl.BlockSpec((tb, ns), lambda g, i: (i, g)),
            pl.BlockSpec((tb, ns), lambda g, i: (i, M2_GROUPS + g)),
            pl.BlockSpec((tb, DT_PAD), lambda g, i: (i, 0)),
            pl.BlockSpec((tb, DT_PAD), lambda g, i: (i, 0)),
            pl.BlockSpec((1, gw), lambda g, i: (0, g)),
            pl.BlockSpec((1, gw), lambda g, i: (0, g)),
        ],
        out_specs=pl.BlockSpec((tb, gw), lambda g, i: (i, g)),
        out_shape=jax.ShapeDtypeStruct((s, D_M2), BF16),
        scratch_shapes=[pltpu.VMEM((ns, gw), F32),
                        pltpu.VMEM((tb // CHUNK, ns, gw), BF16),
                        pltpu.VMEM((tb, gw), F32)],
        compiler_params=_cparams(2),
        name="mamba2_branch",
    )(z, xs, bc, bc, dt, cum, d_x, norm_g.reshape(1, D_M2))


def _xattn_kernel(q_ref, k_ref, v_ref, o_ref):
    scale = XA_HEADDIM ** -0.5
    for h in range(XA_HEADS):
        lo = h * XA_HEADDIM
        hi = lo + XA_HEADDIM
        sc = _dot_nt(q_ref[:, lo:hi], k_ref[:, lo:hi]) * scale
        e = jnp.exp(sc - jnp.max(sc, axis=-1, keepdims=True))
        p = e / jnp.sum(e, axis=-1, keepdims=True)
        o_ref[:, lo:hi] = _dot(p.astype(BF16), v_ref[:, lo:hi]).astype(o_ref.dtype)


def cross_attention(q, kv, *, tm=512):
    s = q.shape[0]
    n_mem = kv.shape[0]
    tm = min(tm, s)
    return pl.pallas_call(
        _xattn_kernel,
        grid=(s // tm,),
        in_specs=[
            pl.BlockSpec((tm, D_MODEL), lambda i: (i, 0)),
            pl.BlockSpec((n_mem, D_MODEL), lambda i: (0, 0)),
            pl.BlockSpec((n_mem, D_MODEL), lambda i: (0, 1)),
        ],
        out_specs=pl.BlockSpec((tm, D_MODEL), lambda i: (i, 0)),
        out_shape=jax.ShapeDtypeStruct((s, D_MODEL), BF16),
        compiler_params=_cparams(1),
        name="cross_attention",
    )(q, kv, kv)


def _final_norm_kernel(x_ref, g_ref, o_ref):
    o_ref[...] = _rms(x_ref[...], g_ref[...], NORM_EPS)


def final_norm(x, g, *, tm=512):
    m, k = x.shape
    tm = min(tm, m)
    return pl.pallas_call(
        _final_norm_kernel,
        grid=(m // tm,),
        in_specs=[pl.BlockSpec((tm, k), lambda i: (i, 0)),
                  pl.BlockSpec((1, k), lambda i: (0, 0))],
        out_specs=pl.BlockSpec((tm, k), lambda i: (i, 0)),
        out_shape=jax.ShapeDtypeStruct((m, k), F32),
        compiler_params=_cparams(1),
        name="final_norm",
    )(x, g.reshape(1, k))


def _per_head_lanes(p):
    return jnp.repeat(p, M2_HEADDIM).reshape(1, D_M2)


def kernel(x, mem, mix_norm_g, w_in, hg_lb_logits, hg_norm_g, m2_conv_w, m2_conv_b, m2_dt_bias, m2_A_log, m2_D, m2_norm_g, w_branch_hg, w_branch_m2, w_out, mem_norm_g, xa_norm_g, xa_wq, xa_wkv, xa_wo, ffn_norm_g, ffn_w_up, ffn_conv_w, ffn_conv_b, ffn_w_down, final_norm_g):
    bsz, seq, d = x.shape
    depth = w_in.shape[0]
    w_plain = jnp.concatenate([w_in[:, :, :IN_MZ], w_in[:, :, IN_GM2:]], axis=2).astype(BF16)
    w_dual = jnp.concatenate([w_in[:, :, IN_MZ:IN_DT], w_in[:, :, IN_GHG:IN_GM2]], axis=2).astype(BF16)
    w_bhg = w_branch_hg.astype(BF16)
    w_bm2 = w_branch_m2.astype(BF16)
    w_o = w_out.astype(BF16)
    wq = xa_wq.astype(BF16)
    wkv = xa_wkv.astype(BF16)
    wo = xa_wo.astype(BF16)
    w_up = ffn_w_up.astype(BF16)
    w_down = ffn_w_down.astype(BF16)
    outs = []
    for b in range(bsz):
        h = x[b]
        mem_b = mem[b]
        for l in range(depth):
            g_mix = mix_norm_g[l]
            proj = norm_matmul(h, g_mix, w_plain, l, out_dtype=F32)
            xs, zz = norm_matmul_conv(h, g_mix, w_dual, l, m2_conv_w[l], m2_conv_b[l], n_out=D_M2,
                                      w_col0=DU_X, w2_col0=DU_MZ, glu=False, conv_col0=0,
                                      out_dtype=F32)
            bc, g_hg = norm_matmul_conv(h, g_mix, w_dual, l, m2_conv_w[l], m2_conv_b[l],
                                        n_out=2 * D_BC, w_col0=DU_BC, w2_col0=DU_GHG, glu=False,
                                        conv_col0=D_M2, out_dtype=BF16)
            dt, cum = dt_path(h, g_mix, w_in, l, m2_dt_bias[l], m2_A_log[l])
            o_hg = hgrn2_branch(proj, hg_lb_logits, hg_norm_g[l], l)
            o_m2 = mamba2_branch(zz, xs, bc, dt, cum, _per_head_lanes(m2_D[l]), m2_norm_g[l])
            merged = merge_branches(o_hg, w_bhg, o_m2, w_bm2, l, g_hg, proj)
            h = matmul_residual(merged, w_o, l, h, tm=512, tn=D_MODEL)

            q = norm_matmul(h, xa_norm_g[l], wq, l, out_dtype=BF16)
            kv = norm_matmul(mem_b, mem_norm_g, wkv, l, out_dtype=BF16)
            att = cross_attention(q, kv)
            h = matmul_residual(att, wo, l, h, tm=512, tn=D_MODEL)

            act = norm_matmul_conv(h, ffn_norm_g[l], w_up, l, ffn_conv_w[l], ffn_conv_b[l],
                                   n_out=D_FF, w_col0=0, w2_col0=D_FF, glu=True, out_dtype=BF16)
            h = matmul_residual(act, w_down, l, h, tm=1024, tn=512)
        outs.append(final_norm(h, final_norm_g))
    return jnp.stack(outs, axis=0)
```

```python
import functools

import jax
import jax.numpy as jnp
from jax import lax
from jax.experimental import pallas as pl
from jax.experimental.pallas import tpu as pltpu

F32 = jnp.float32
BF16 = jnp.bfloat16

D_MODEL = 2048
HG_HEADS = 16
HG_KDIM = 128
HG_VDIM = 128
D_HG = HG_HEADS * HG_KDIM
M2_HEADS = 32
M2_HEADDIM = 64
D_M2 = M2_HEADS * M2_HEADDIM
M2_GROUPS = 8
M2_DSTATE = 128
M2_CONV = 4
M2_GROUP_W = D_M2 // M2_GROUPS
M2_HPG = M2_HEADS // M2_GROUPS
D_BC = M2_GROUPS * M2_DSTATE
CHUNK = 64
HG_SUB = 32
HG_HPS = 4
XA_HEADS = 4
XA_HEADDIM = D_MODEL // XA_HEADS
D_FF = 5632
FFN_CONV = 3
NORM_EPS = 1e-6
M2_NORM_EPS = 1e-5
LB_FLOOR = 1e-30
HG_SAFE_DECAY = 80.0

IN_DT = 4 * D_HG + 2 * D_M2 + 2 * D_BC
IN_GHG = IN_DT + M2_HEADS
OFF_HQ = 0
OFF_HF = OFF_HQ + D_HG
OFF_HI = OFF_HF + D_HG
OFF_HOG = OFF_HI + D_HG
OFF_MZ = OFF_HOG + D_HG
OFF_MX = OFF_MZ + D_M2
OFF_MB = OFF_MX + D_M2
OFF_MC = OFF_MB + D_BC
OFF_GHG = OFF_MC + D_BC
OFF_GM2 = OFF_GHG + D_MODEL
DT_PAD = 128

HALO = 8
VMEM_LIMIT = 56 * 1024 * 1024


def _cparams(n_axes):
    return pltpu.CompilerParams(
        dimension_semantics=("arbitrary",) * n_axes,
        vmem_limit_bytes=VMEM_LIMIT)


def _rms(x, g, eps):
    return x * lax.rsqrt(jnp.mean(x * x, axis=-1, keepdims=True) + eps) * g


def _sigmoid(x):
    return 0.5 * jnp.tanh(0.5 * x) + 0.5


def _silu(x):
    hx = 0.5 * x
    return hx * jnp.tanh(hx) + hx


def _softplus(x):
    return jnp.maximum(x, 0.0) + jnp.log1p(jnp.exp(-jnp.abs(x)))


def _dot(a, b):
    return jnp.dot(a, b, preferred_element_type=F32)


def _dot_nt(a, b):
    return lax.dot_general(a, b, (((1,), (1,)), ((), ())), preferred_element_type=F32)


def _dot_tn(a, b):
    return lax.dot_general(a, b, (((0,), (0,)), ((), ())), preferred_element_type=F32)


def _split3(x):
    hi = x.astype(BF16)
    r1 = x - hi.astype(F32)
    mid = r1.astype(BF16)
    lo = (r1 - mid.astype(F32)).astype(BF16)
    return hi, mid, lo


def _seg_cumsum(x, seg):
    row = lax.broadcasted_iota(jnp.int32, x.shape, 0) % seg
    k = 1
    while k < seg:
        x = x + jnp.where(row >= k, pltpu.roll(x, k, axis=0), 0.0)
        k *= 2
    return x


def _wspec(k, tn, layer, col0=0):
    return pl.BlockSpec((None, k, tn), lambda i, j: (layer, 0, col0 + j))


def _norm_matmul_kernel(a_ref, g_ref, b_ref, o_ref, an_ref):
    @pl.when(pl.program_id(1) == 0)
    def _():
        an_ref[...] = _rms(a_ref[...], g_ref[...], NORM_EPS).astype(BF16)

    o_ref[...] = _dot(an_ref[...], b_ref[...]).astype(o_ref.dtype)


def norm_matmul(a, g, w, layer, *, out_dtype, tm=1024, tn=1024):
    m, k = a.shape
    n = w.shape[2]
    tm = min(tm, m)
    tn = min(tn, n)
    return pl.pallas_call(
        _norm_matmul_kernel,
        grid=(m // tm, n // tn),
        in_specs=[
            pl.BlockSpec((tm, k), lambda i, j: (i, 0)),
            pl.BlockSpec((1, k), lambda i, j: (0, 0)),
            _wspec(k, tn, layer),
        ],
        out_specs=pl.BlockSpec((tm, tn), lambda i, j: (i, j)),
        out_shape=jax.ShapeDtypeStruct((m, n), out_dtype),
        scratch_shapes=[pltpu.VMEM((tm, k), BF16)],
        compiler_params=_cparams(2),
        name="norm_matmul",
    )(a, g.reshape(1, k), w)


def _ffn_up_kernel(a_ref, g_ref, wg_ref, wu_ref, cw_ref, cb_ref, o_ref, an_ref, halo_ref, ext_ref,
                   *, tm):
    i = pl.program_id(0)
    j = pl.program_id(1)

    @pl.when(jnp.logical_and(i == 0, j == 0))
    def _():
        halo_ref[...] = jnp.zeros_like(halo_ref)

    @pl.when(j == 0)
    def _():
        an_ref[...] = _rms(a_ref[...], g_ref[...], NORM_EPS).astype(BF16)

    an = an_ref[...]
    ext_ref[0:HALO, :] = halo_ref[j]
    ext_ref[HALO:HALO + tm, :] = _dot(an, wg_ref[...])
    w = cw_ref[...]
    acc = cb_ref[...] + w[FFN_CONV - 1:FFN_CONV, :] * ext_ref[HALO:HALO + tm, :]
    for t in range(1, FFN_CONV):
        acc = acc + w[FFN_CONV - 1 - t:FFN_CONV - t, :] * ext_ref[HALO - t:HALO - t + tm, :]
    halo_ref[j] = ext_ref[tm:tm + HALO, :]
    gelu = 0.5 * acc * (1.0 + lax.erf(acc * (2.0 ** -0.5)))
    o_ref[...] = (gelu * _dot(an, wu_ref[...])).astype(o_ref.dtype)


def ffn_up(a, g, w_up, layer, conv_w, conv_b, *, tm=1024, tn=512):
    m, k = a.shape
    tm = min(tm, m)
    n_col = D_FF // tn
    kern = functools.partial(_ffn_up_kernel, tm=tm)
    return pl.pallas_call(
        kern,
        grid=(m // tm, n_col),
        in_specs=[
            pl.BlockSpec((tm, k), lambda i, j: (i, 0)),
            pl.BlockSpec((1, k), lambda i, j: (0, 0)),
            _wspec(k, tn, layer),
            _wspec(k, tn, layer, n_col),
            pl.BlockSpec((FFN_CONV, tn), lambda i, j: (0, j)),
            pl.BlockSpec((1, tn), lambda i, j: (0, j)),
        ],
        out_specs=pl.BlockSpec((tm, tn), lambda i, j: (i, j)),
        out_shape=jax.ShapeDtypeStruct((m, D_FF), BF16),
        scratch_shapes=[pltpu.VMEM((tm, k), BF16),
                        pltpu.VMEM((n_col, HALO, tn), F32),
                        pltpu.VMEM((tm + HALO, tn), F32)],
        compiler_params=_cparams(2),
        name="ffn_up",
    )(a, g.reshape(1, k), w_up, w_up, conv_w, conv_b.reshape(1, D_FF))


def _matmul_res_kernel(a_ref, b_ref, r_ref, o_ref):
    o_ref[...] = r_ref[...] + _dot(a_ref[...], b_ref[...])


def matmul_residual(a, w, layer, r, *, tm, tn):
    m, k = a.shape
    n = w.shape[2]
    tm = min(tm, m)
    tn = min(tn, n)
    return pl.pallas_call(
        _matmul_res_kernel,
        grid=(m // tm, n // tn),
        in_specs=[
            pl.BlockSpec((tm, k), lambda i, j: (i, 0)),
            _wspec(k, tn, layer),
            pl.BlockSpec((tm, tn), lambda i, j: (i, j)),
        ],
        out_specs=pl.BlockSpec((tm, tn), lambda i, j: (i, j)),
        out_shape=jax.ShapeDtypeStruct((m, n), F32),
        compiler_params=_cparams(2),
        name="matmul_residual",
    )(a, w, r)


def _merge_kernel(a1_ref, b1_ref, a2_ref, b2_ref, g1_ref, g2_ref, o_ref):
    y1 = _dot(a1_ref[...], b1_ref[...])
    y2 = _dot(a2_ref[...], b2_ref[...])
    o_ref[...] = (_sigmoid(g1_ref[...]) * y1 + _sigmoid(g2_ref[...]) * y2).astype(o_ref.dtype)


def merge_branches(o_hg, w_hg, o_m2, w_m2, layer, proj, *, tm=512, tn=1024):
    m, k = o_hg.shape
    n = w_hg.shape[2]
    tm = min(tm, m)
    g1 = OFF_GHG // tn
    g2 = OFF_GM2 // tn
    wspec = pl.BlockSpec((None, k, tn), lambda j, i: (layer, 0, j))
    return pl.pallas_call(
        _merge_kernel,
        grid=(n // tn, m // tm),
        in_specs=[
            pl.BlockSpec((tm, k), lambda j, i: (i, 0)),
            wspec,
            pl.BlockSpec((tm, k), lambda j, i: (i, 0)),
            wspec,
            pl.BlockSpec((tm, tn), lambda j, i: (i, g1 + j)),
            pl.BlockSpec((tm, tn), lambda j, i: (i, g2 + j)),
        ],
        out_specs=pl.BlockSpec((tm, tn), lambda j, i: (i, j)),
        out_shape=jax.ShapeDtypeStruct((m, n), BF16),
        compiler_params=_cparams(2),
        name="merge_branches",
    )(o_hg, w_hg, o_m2, w_m2, proj, proj)


def _hgrn2_kernel(q_ref, f_ref, i_ref, og_ref, lbl_ref, ng_ref, o_ref,
                  st_ref, sc_ref, stb_ref, bl_s, q_s, k_s, *, layer, tb):
    n_sub = tb // HG_SUB
    width = HG_HPS * HG_KDIM

    @pl.when(pl.program_id(1) == 0)
    def _():
        st_ref[...] = jnp.zeros_like(st_ref)

    logits = lbl_ref[...]
    e = jnp.exp(logits - jnp.max(logits, axis=0, keepdims=True))
    p = e / jnp.sum(e, axis=0, keepdims=True)
    lb = jnp.zeros((1, width), F32)
    for i in range(1, layer + 1):
        lb = lb + p[i:i + 1, :]
    lbf = jnp.maximum(lb, LB_FLOOR)

    z = f_ref[...]
    t = jnp.exp(-jnp.abs(z))
    r = 1.0 / (1.0 + t)
    pos = z >= 0.0
    log_f = jnp.log(jnp.where(pos, 1.0 + lbf * t, t + lbf) * r)
    kk = (1.0 - lb) * jnp.where(pos, t, 1.0) * r
    q = q_ref[...] * (HG_KDIM ** -0.5)
    v = i_ref[...].astype(BF16)

    bl = _seg_cumsum(log_f, HG_SUB)
    qt = (q * jnp.exp(bl)).astype(BF16)
    kinc = (kk * jnp.exp(-bl)).astype(BF16)
    worst = jnp.max(-bl)

    def blk(a, n, h):
        return a[n * HG_SUB:(n + 1) * HG_SUB, h * HG_KDIM:(h + 1) * HG_KDIM]

    for h in range(HG_HPS):
        for n in range(n_sub):
            sc_ref[h * n_sub + n] = _dot_nt(blk(qt, n, h), blk(kinc, n, h))

    for h in range(HG_HPS):
        st = st_ref[h]
        for n in range(n_sub):
            blc = blk(bl, n, h)
            b_last = blc[HG_SUB - 1:HG_SUB, :]
            kdec = (blk(kk, n, h) * jnp.exp(b_last - blc)).astype(BF16)
            stb_ref[h * n_sub + n] = st.astype(BF16)
            st = st * jnp.exp(b_last) + _dot_tn(blk(v, n, h), kdec)
        st_ref[h] = st

    col_id = lax.broadcasted_iota(jnp.int32, (HG_SUB, HG_SUB), 1)
    tri = lax.broadcasted_iota(jnp.int32, (HG_SUB, HG_SUB), 0) >= col_id

    @pl.when(worst > HG_SAFE_DECAY)
    def _():
        for h in range(HG_HPS):
            cols = slice(h * HG_KDIM, (h + 1) * HG_KDIM)
            bl_s[h * tb:(h + 1) * tb, :] = bl[:, cols]
            q_s[h * tb:(h + 1) * tb, :] = q[:, cols]
            k_s[h * tb:(h + 1) * tb, :] = kk[:, cols]

        def body(n, carry):
            off = pl.multiple_of(n * HG_SUB, HG_SUB)
            blc = bl_s[pl.ds(off, HG_SUB), :]
            qc = q_s[pl.ds(off, HG_SUB), :]
            kc = k_s[pl.ds(off, HG_SUB), :]
            sc = jnp.zeros((HG_SUB, HG_SUB), F32)
            for s in range(HG_SUB):
                d = jnp.exp(jnp.minimum(blc - blc[s:s + 1, :], 0.0))
                col = jnp.sum(qc * d * kc[s:s + 1, :], axis=-1, keepdims=True)
                sc = jnp.where(col_id == s, col, sc)
            sc_ref[n] = sc
            return carry

        lax.fori_loop(0, HG_HPS * n_sub, body, 0)

    for h in range(HG_HPS):
        cols = slice(h * HG_KDIM, (h + 1) * HG_KDIM)
        outs = []
        for n in range(n_sub):
            sc = jnp.where(tri, sc_ref[h * n_sub + n], 0.0).astype(BF16)
            outs.append(_dot(sc, blk(v, n, h)) + _dot_nt(blk(qt, n, h), stb_ref[h * n_sub + n]))
        o = jnp.concatenate(outs, axis=0) * _sigmoid(og_ref[:, cols])
        o_ref[:, cols] = _rms(o, ng_ref[:, cols], NORM_EPS).astype(o_ref.dtype)


def hgrn2_branch(proj, lb_logits, norm_g, layer, *, tb=512):
    s = proj.shape[0]
    tb = min(tb, s)
    depth = lb_logits.shape[0]
    n_sub = tb // HG_SUB
    width = HG_HPS * HG_KDIM
    kern = functools.partial(_hgrn2_kernel, layer=layer, tb=tb)

    def col(off):
        base = off // width
        return pl.BlockSpec((tb, width), lambda h, i: (i, base + h))

    return pl.pallas_call(
        kern,
        grid=(HG_HEADS // HG_HPS, s // tb),
        in_specs=[
            col(OFF_HQ), col(OFF_HF), col(OFF_HI), col(OFF_HOG),
            pl.BlockSpec((depth, width), lambda h, i: (0, h)),
            pl.BlockSpec((1, width), lambda h, i: (0, h)),
        ],
        out_specs=pl.BlockSpec((tb, width), lambda h, i: (i, h)),
        out_shape=jax.ShapeDtypeStruct((s, D_HG), BF16),
        scratch_shapes=[pltpu.VMEM((HG_HPS, HG_VDIM, HG_KDIM), F32),
                        pltpu.VMEM((HG_HPS * n_sub, HG_SUB, HG_SUB), F32),
                        pltpu.VMEM((HG_HPS * n_sub, HG_VDIM, HG_KDIM), BF16),
                        pltpu.VMEM((HG_HPS * tb, HG_KDIM), F32),
                        pltpu.VMEM((HG_HPS * tb, HG_KDIM), F32),
                        pltpu.VMEM((HG_HPS * tb, HG_KDIM), F32)],
        compiler_params=_cparams(2),
        name="hgrn2_branch",
    )(proj, proj, proj, proj, lb_logits, norm_g.reshape(1, D_HG))


def _dt_kernel(a_ref, g_ref, w_ref, bias_ref, alog_ref, dt_ref, cum_ref):
    u = _rms(a_ref[...], g_ref[...], NORM_EPS).astype(BF16)
    dt = _softplus(_dot(u, w_ref[...]) + bias_ref[...])
    dt_ref[...] = dt
    cum_ref[...] = _seg_cumsum(dt * (-jnp.exp(alog_ref[...])), CHUNK)


def dt_path(a, g, w_dt, layer, dt_bias, a_log, *, tm=1024):
    m, k = a.shape
    tm = min(tm, m)

    def pad(p):
        return jnp.pad(p, (0, DT_PAD - M2_HEADS)).reshape(1, DT_PAD)

    out = jax.ShapeDtypeStruct((m, DT_PAD), F32)
    return pl.pallas_call(
        _dt_kernel,
        grid=(m // tm,),
        in_specs=[
            pl.BlockSpec((tm, k), lambda i: (i, 0)),
            pl.BlockSpec((1, k), lambda i: (0, 0)),
            pl.BlockSpec((None, k, DT_PAD), lambda i: (layer, 0, 0)),
            pl.BlockSpec((1, DT_PAD), lambda i: (0, 0)),
            pl.BlockSpec((1, DT_PAD), lambda i: (0, 0)),
        ],
        out_specs=[pl.BlockSpec((tm, DT_PAD), lambda i: (i, 0))] * 2,
        out_shape=[out, out],
        compiler_params=_cparams(1),
        name="dt_path",
    )(a, g.reshape(1, k), w_dt, pad(dt_bias), pad(a_log))


def _mamba2_kernel(z_ref, x_ref, b_ref, c_ref, dt_ref, cum_ref, wx_ref, wb_ref, wc_ref,
                   bx_ref, bb_ref, bc_ref, dsk_ref, ng_ref,
                   o_ref, ext_ref, st_ref, stb_ref, y_ref, *, tb):
    g = pl.program_id(0)
    gw = M2_GROUP_W
    ns = M2_DSTATE
    w_all = gw + 2 * ns
    n_chunk = tb // CHUNK

    @pl.when(pl.program_id(1) == 0)
    def _():
        st_ref[...] = jnp.zeros_like(st_ref)
        ext_ref[0:HALO, :] = jnp.zeros((HALO, w_all), F32)

    ext_ref[HALO:HALO + tb, 0:gw] = x_ref[...]
    ext_ref[HALO:HALO + tb, gw:gw + ns] = b_ref[...]
    ext_ref[HALO:HALO + tb, gw + ns:w_all] = c_ref[...]
    w = jnp.concatenate([wx_ref[...], wb_ref[...], wc_ref[...]], axis=1)
    bias = jnp.concatenate([bx_ref[...], bb_ref[...], bc_ref[...]], axis=1)
    acc = bias + w[M2_CONV - 1:M2_CONV, :] * ext_ref[HALO:HALO + tb, :]
    for j in range(1, M2_CONV):
        acc = acc + w[M2_CONV - 1 - j:M2_CONV - j, :] * ext_ref[HALO - j:HALO - j + tb, :]
    ext_ref[0:HALO, :] = ext_ref[tb:tb + HALO, :]
    xbc = _silu(acc)
    xs = xbc[:, 0:gw]
    bm = xbc[:, gw:gw + ns].astype(BF16)
    cm = xbc[:, gw + ns:w_all].astype(BF16)

    r_id = lax.broadcasted_iota(jnp.int32, (DT_PAD, gw), 0)
    c_id = lax.broadcasted_iota(jnp.int32, (DT_PAD, gw), 1)
    expand = (r_id == g * M2_HPG + c_id // M2_HEADDIM).astype(BF16)
    d_parts = _split3(dt_ref[...])
    c_parts = _split3(cum_ref[...])
    dtx = _dot(d_parts[0], expand) + _dot(d_parts[1], expand) + _dot(d_parts[2], expand)
    cumx = _dot(c_parts[0], expand) + _dot(c_parts[1], expand) + _dot(c_parts[2], expand)
    xdt = xs * dtx
    ecum = jnp.exp(cumx)

    sr = lax.broadcasted_iota(jnp.int32, (HALO, DT_PAD), 0)
    sl = lax.broadcasted_iota(jnp.int32, (HALO, DT_PAD), 1)
    sel = jnp.logical_and(sr < M2_HPG, sl == g * M2_HPG + sr).astype(BF16)

    def rot(a):
        return jnp.concatenate([a[tb - CHUNK:, :], a[:tb - CHUNK, :]], axis=0)

    rows_all = _dot_nt(sel, c_parts[0]) + _dot_nt(sel, c_parts[1]) + _dot_nt(sel, c_parts[2])
    rows_rot = (_dot_nt(sel, rot(c_parts[0])) + _dot_nt(sel, rot(c_parts[1]))
                + _dot_nt(sel, rot(c_parts[2])))

    pair_w = 2 * CHUNK
    low_half = lax.broadcasted_iota(jnp.int32, (1, pair_w), 1) < CHUNK
    tri = (lax.broadcasted_iota(jnp.int32, (CHUNK, pair_w), 0)
           >= lax.broadcasted_iota(jnp.int32, (CHUNK, pair_w), 1) % CHUNK)
    head_of_col = lax.broadcasted_iota(jnp.int32, (1, gw), 1) // M2_HEADDIM

    def rows(a, n):
        return a[n * CHUNK:(n + 1) * CHUNK, :]

    for n in range(n_chunk):
        cb4 = _dot_nt(rows(cm, n), jnp.concatenate([rows(bm, n)] * M2_HPG, axis=0))
        cumc = rows(cumx, n)
        lparts = []
        for p_ in range(M2_HPG // 2):
            if n % 2 == 0:
                a0 = n * CHUNK
                rp = jnp.where(low_half, rows_all[2 * p_:2 * p_ + 1, a0:a0 + pair_w],
                               rows_rot[2 * p_ + 1:2 * p_ + 2, a0:a0 + pair_w])
            else:
                a_lo = ((n + 1) * CHUNK) % tb
                a_hi = (n - 1) * CHUNK
                rp = jnp.where(low_half, rows_rot[2 * p_:2 * p_ + 1, a_lo:a_lo + pair_w],
                               rows_all[2 * p_ + 1:2 * p_ + 2, a_hi:a_hi + pair_w])
            diff = cumc[:, p_ * pair_w:(p_ + 1) * pair_w] - rp
            lparts.append(jnp.where(tri, jnp.exp(jnp.where(tri, diff, 0.0)), 0.0))
        lhs = (cb4 * jnp.concatenate(lparts, axis=1)).astype(BF16)
        xdtb = rows(xdt, n).astype(BF16)
        rhs = jnp.concatenate(
            [jnp.where(head_of_col == h, xdtb, jnp.zeros_like(xdtb)) for h in range(M2_HPG)], axis=0)
        y_ref[n * CHUNK:(n + 1) * CHUNK, :] = _dot(lhs, rhs)

    st = st_ref[...]
    for n in range(n_chunk):
        cumc = rows(cumx, n)
        last = cumc[CHUNK - 1:CHUNK, :]
        wdec = (rows(xdt, n) * jnp.exp(last - cumc)).astype(BF16)
        stb_ref[n] = st.astype(BF16)
        st = st * jnp.exp(last) + _dot_tn(rows(bm, n), wdec)
    st_ref[...] = st

    for n in range(n_chunk):
        y_ref[n * CHUNK:(n + 1) * CHUNK, :] += _dot(rows(cm, n), stb_ref[n]) * rows(ecum, n)

    y = (y_ref[...] + dsk_ref[...] * xs) * _silu(z_ref[...])
    o_ref[...] = _rms(y, ng_ref[...], M2_NORM_EPS).astype(o_ref.dtype)


def mamba2_branch(proj, dt, cum, conv_w, conv_b, d_x, norm_g, *, tb=512):
    s = proj.shape[0]
    tb = min(tb, s)
    kern = functools.partial(_mamba2_kernel, tb=tb)
    gw = M2_GROUP_W
    ns = M2_DSTATE
    w_all = gw + 2 * ns

    def colw(off, width):
        base = off // width
        return pl.BlockSpec((tb, width), lambda g, i: (i, base + g))

    def par(rows, off, width):
        base = off // width
        return pl.BlockSpec((rows, width), lambda g, i: (0, base + g))

    conv_b = conv_b.reshape(1, -1)
    return pl.pallas_call(
        kern,
        grid=(M2_GROUPS, s // tb),
        in_specs=[
            colw(OFF_MZ, gw), colw(OFF_MX, gw), colw(OFF_MB, ns), colw(OFF_MC, ns),
            pl.BlockSpec((tb, DT_PAD), lambda g, i: (i, 0)),
            pl.BlockSpec((tb, DT_PAD), lambda g, i: (i, 0)),
            par(M2_CONV, 0, gw), par(M2_CONV, D_M2, ns), par(M2_CONV, D_M2 + D_BC, ns),
            par(1, 0, gw), par(1, D_M2, ns), par(1, D_M2 + D_BC, ns),
            par(1, 0, gw), par(1, 0, gw),
        ],
        out_specs=pl.BlockSpec((tb, gw), lambda g, i: (i, g)),
        out_shape=jax.ShapeDtypeStruct((s, D_M2), BF16),
        scratch_shapes=[pltpu.VMEM((tb + HALO, w_all), F32),
                        pltpu.VMEM((ns, gw), F32),
                        pltpu.VMEM((tb // CHUNK, ns, gw), BF16),
                        pltpu.VMEM((tb, gw), F32)],
        compiler_params=_cparams(2),
        name="mamba2_branch",
    )(proj, proj, proj, proj, dt, cum, conv_w, conv_w, conv_w, conv_b, conv_b, conv_b,
      d_x, norm_g.reshape(1, D_M2))


def _xattn_kernel(q_ref, k_ref, v_ref, o_ref):
    scale = XA_HEADDIM ** -0.5
    for h in range(XA_HEADS):
        lo = h * XA_HEADDIM
        hi = lo + XA_HEADDIM
        sc = _dot_nt(q_ref[:, lo:hi], k_ref[:, lo:hi]) * scale
        e = jnp.exp(sc - jnp.max(sc, axis=-1, keepdims=True))
        p = e / jnp.sum(e, axis=-1, keepdims=True)
        o_ref[:, lo:hi] = _dot(p.astype(BF16), v_ref[:, lo:hi]).astype(o_ref.dtype)


def cross_attention(q, kv, *, tm=512):
    s = q.shape[0]
    n_mem = kv.shape[0]
    tm = min(tm, s)
    return pl.pallas_call(
        _xattn_kernel,
        grid=(s // tm,),
        in_specs=[
            pl.BlockSpec((tm, D_MODEL), lambda i: (i, 0)),
            pl.BlockSpec((n_mem, D_MODEL), lambda i: (0, 0)),
            pl.BlockSpec((n_mem, D_MODEL), lambda i: (0, 1)),
        ],
        out_specs=pl.BlockSpec((tm, D_MODEL), lambda i: (i, 0)),
        out_shape=jax.ShapeDtypeStruct((s, D_MODEL), BF16),
        compiler_params=_cparams(1),
        name="cross_attention",
    )(q, kv, kv)


def _final_norm_kernel(x_ref, g_ref, o_ref):
    o_ref[...] = _rms(x_ref[...], g_ref[...], NORM_EPS)


def final_norm(x, g, *, tm=512):
    m, k = x.shape
    tm = min(tm, m)
    return pl.pallas_call(
        _final_norm_kernel,
        grid=(m // tm,),
        in_specs=[pl.BlockSpec((tm, k), lambda i: (i, 0)),
                  pl.BlockSpec((1, k), lambda i: (0, 0))],
        out_specs=pl.BlockSpec((tm, k), lambda i: (i, 0)),
        out_shape=jax.ShapeDtypeStruct((m, k), F32),
        compiler_params=_cparams(1),
        name="final_norm",
    )(x, g.reshape(1, k))


def _per_head_lanes(p):
    return jnp.repeat(p, M2_HEADDIM).reshape(1, D_M2)


def kernel(x, mem, mix_norm_g, w_in, hg_lb_logits, hg_norm_g, m2_conv_w, m2_conv_b, m2_dt_bias, m2_A_log, m2_D, m2_norm_g, w_branch_hg, w_branch_m2, w_out, mem_norm_g, xa_norm_g, xa_wq, xa_wkv, xa_wo, ffn_norm_g, ffn_w_up, ffn_conv_w, ffn_conv_b, ffn_w_down, final_norm_g):
    bsz, seq, d = x.shape
    depth = w_in.shape[0]
    w_proj = jnp.concatenate([w_in[:, :, :IN_DT], w_in[:, :, IN_GHG:]], axis=2).astype(BF16)
    w_dt = w_in[:, :, IN_DT:IN_DT + DT_PAD].astype(BF16)
    w_bhg = w_branch_hg.astype(BF16)
    w_bm2 = w_branch_m2.astype(BF16)
    w_o = w_out.astype(BF16)
    wq = xa_wq.astype(BF16)
    wkv = xa_wkv.astype(BF16)
    wo = xa_wo.astype(BF16)
    w_up = ffn_w_up.astype(BF16)
    w_down = ffn_w_down.astype(BF16)
    outs = []
    for b in range(bsz):
        h = x[b]
        mem_b = mem[b]
        for l in range(depth):
            g_mix = mix_norm_g[l]
            proj = norm_matmul(h, g_mix, w_proj, l, out_dtype=F32, tn=1536)
            dt, cum = dt_path(h, g_mix, w_dt, l, m2_dt_bias[l], m2_A_log[l])
            o_hg = hgrn2_branch(proj, hg_lb_logits, hg_norm_g[l], l)
            o_m2 = mamba2_branch(proj, dt, cum, m2_conv_w[l], m2_conv_b[l],
                                 _per_head_lanes(m2_D[l]), m2_norm_g[l])
            merged = merge_branches(o_hg, w_bhg, o_m2, w_bm2, l, proj)
            h = matmul_residual(merged, w_o, l, h, tm=512, tn=D_MODEL)

            q = norm_matmul(h, xa_norm_g[l], wq, l, out_dtype=BF16)
            kv = norm_matmul(mem_b, mem_norm_g, wkv, l, out_dtype=BF16)
            att = cross_attention(q, kv)
            h = matmul_residual(att, wo, l, h, tm=512, tn=D_MODEL)

            act = ffn_up(h, ffn_norm_g[l], w_up, l, ffn_conv_w[l], ffn_conv_b[l])
            h = matmul_residual(act, w_down, l, h, tm=1024, tn=512)
        outs.append(final_norm(h, final_norm_g))
    return jnp.stack(outs, axis=0)
```

```python
import functools

import jax
import jax.numpy as jnp
from jax import lax
from jax.experimental import pallas as pl
from jax.experimental.pallas import tpu as pltpu

F32 = jnp.float32
BF16 = jnp.bfloat16

D_MODEL = 2048
HG_HEADS = 16
HG_KDIM = 128
HG_VDIM = 128
D_HG = HG_HEADS * HG_KDIM
M2_HEADS = 32
M2_HEADDIM = 64
D_M2 = M2_HEADS * M2_HEADDIM
M2_GROUPS = 8
M2_DSTATE = 128
M2_CONV = 4
M2_GROUP_W = D_M2 // M2_GROUPS
M2_HPG = M2_HEADS // M2_GROUPS
D_BC = M2_GROUPS * M2_DSTATE
CHUNK = 64
HG_SUB = 32
HG_HPS = 4
XA_HEADS = 4
XA_HEADDIM = D_MODEL // XA_HEADS
D_FF = 5632
FFN_CONV = 3
NORM_EPS = 1e-6
M2_NORM_EPS = 1e-5
LB_FLOOR = 1e-30
HG_SAFE_DECAY = 80.0

IN_DT = 4 * D_HG + 2 * D_M2 + 2 * D_BC
IN_GHG = IN_DT + M2_HEADS
OFF_HQ = 0
OFF_HF = OFF_HQ + D_HG
OFF_HI = OFF_HF + D_HG
OFF_HOG = OFF_HI + D_HG
OFF_MZ = OFF_HOG + D_HG
OFF_MX = OFF_MZ + D_M2
OFF_MB = OFF_MX + D_M2
OFF_MC = OFF_MB + D_BC
OFF_GHG = OFF_MC + D_BC
OFF_GM2 = OFF_GHG + D_MODEL
DT_PAD = 128

HALO = 8
VMEM_LIMIT = 56 * 1024 * 1024


def _cparams(n_axes):
    return pltpu.CompilerParams(
        dimension_semantics=("arbitrary",) * n_axes,
        vmem_limit_bytes=VMEM_LIMIT)


def _rms(x, g, eps):
    return x * lax.rsqrt(jnp.mean(x * x, axis=-1, keepdims=True) + eps) * g


def _sigmoid(x):
    return 0.5 * jnp.tanh(0.5 * x) + 0.5


def _silu(x):
    hx = 0.5 * x
    return hx * jnp.tanh(hx) + hx


def _softplus(x):
    return jnp.maximum(x, 0.0) + jnp.log1p(jnp.exp(-jnp.abs(x)))


def _dot(a, b):
    return jnp.dot(a, b, preferred_element_type=F32)


def _dot_nt(a, b):
    return lax.dot_general(a, b, (((1,), (1,)), ((), ())), preferred_element_type=F32)


def _dot_tn(a, b):
    return lax.dot_general(a, b, (((0,), (0,)), ((), ())), preferred_element_type=F32)


def _split3(x):
    hi = x.astype(BF16)
    r1 = x - hi.astype(F32)
    mid = r1.astype(BF16)
    lo = (r1 - mid.astype(F32)).astype(BF16)
    return hi, mid, lo


def _seg_cumsum(x, seg):
    row = lax.broadcasted_iota(jnp.int32, x.shape, 0) % seg
    k = 1
    while k < seg:
        x = x + jnp.where(row >= k, pltpu.roll(x, k, axis=0), 0.0)
        k *= 2
    return x


def _wspec(k, tn, layer, col0=0):
    return pl.BlockSpec((None, k, tn), lambda i, j: (layer, 0, col0 + j))


def _norm_matmul_kernel(a_ref, g_ref, b_ref, o_ref, an_ref):
    @pl.when(pl.program_id(1) == 0)
    def _():
        an_ref[...] = _rms(a_ref[...], g_ref[...], NORM_EPS).astype(BF16)

    o_ref[...] = _dot(an_ref[...], b_ref[...]).astype(o_ref.dtype)


def norm_matmul(a, g, w, layer, *, out_dtype, tm=1024, tn=1024):
    m, k = a.shape
    n = w.shape[2]
    tm = min(tm, m)
    tn = min(tn, n)
    return pl.pallas_call(
        _norm_matmul_kernel,
        grid=(m // tm, n // tn),
        in_specs=[
            pl.BlockSpec((tm, k), lambda i, j: (i, 0)),
            pl.BlockSpec((1, k), lambda i, j: (0, 0)),
            _wspec(k, tn, layer),
        ],
        out_specs=pl.BlockSpec((tm, tn), lambda i, j: (i, j)),
        out_shape=jax.ShapeDtypeStruct((m, n), out_dtype),
        scratch_shapes=[pltpu.VMEM((tm, k), BF16)],
        compiler_params=_cparams(2),
        name="norm_matmul",
    )(a, g.reshape(1, k), w)


def _in_proj_kernel(a_ref, g_ref, b_ref, wdt_ref, bias_ref, alog_ref, o_ref, dt_ref, cum_ref, an_ref):
    @pl.when(pl.program_id(1) == 0)
    def _():
        an = _rms(a_ref[...], g_ref[...], NORM_EPS).astype(BF16)
        an_ref[...] = an
        dt = _softplus(_dot(an, wdt_ref[...]) + bias_ref[...])
        dt_ref[...] = dt
        cum_ref[...] = _seg_cumsum(dt * (-jnp.exp(alog_ref[...])), CHUNK)

    o_ref[...] = _dot(an_ref[...], b_ref[...])


def in_proj(a, g, w, w_dt, layer, dt_bias, a_log, *, tm=1024, tn=1536):
    m, k = a.shape
    n = w.shape[2]
    tm = min(tm, m)

    def pad(p):
        return jnp.pad(p, (0, DT_PAD - M2_HEADS)).reshape(1, DT_PAD)

    small = jax.ShapeDtypeStruct((m, DT_PAD), F32)
    small_spec = pl.BlockSpec((tm, DT_PAD), lambda i, j: (i, 0))
    return pl.pallas_call(
        _in_proj_kernel,
        grid=(m // tm, n // tn),
        in_specs=[
            pl.BlockSpec((tm, k), lambda i, j: (i, 0)),
            pl.BlockSpec((1, k), lambda i, j: (0, 0)),
            _wspec(k, tn, layer),
            pl.BlockSpec((None, k, DT_PAD), lambda i, j: (layer, 0, 0)),
            pl.BlockSpec((1, DT_PAD), lambda i, j: (0, 0)),
            pl.BlockSpec((1, DT_PAD), lambda i, j: (0, 0)),
        ],
        out_specs=[pl.BlockSpec((tm, tn), lambda i, j: (i, j)), small_spec, small_spec],
        out_shape=[jax.ShapeDtypeStruct((m, n), F32), small, small],
        scratch_shapes=[pltpu.VMEM((tm, k), BF16)],
        compiler_params=_cparams(2),
        name="in_proj",
    )(a, g.reshape(1, k), w, w_dt, pad(dt_bias), pad(a_log))


def _ffn_up_kernel(a_ref, g_ref, wg_ref, wu_ref, cw_ref, cb_ref, o_ref, an_ref, halo_ref, ext_ref,
                   *, tm):
    i = pl.program_id(0)
    j = pl.program_id(1)

    @pl.when(jnp.logical_and(i == 0, j == 0))
    def _():
        halo_ref[...] = jnp.zeros_like(halo_ref)

    @pl.when(j == 0)
    def _():
        an_ref[...] = _rms(a_ref[...], g_ref[...], NORM_EPS).astype(BF16)

    an = an_ref[...]
    ext_ref[0:HALO, :] = halo_ref[j]
    ext_ref[HALO:HALO + tm, :] = _dot(an, wg_ref[...])
    w = cw_ref[...]
    acc = cb_ref[...] + w[FFN_CONV - 1:FFN_CONV, :] * ext_ref[HALO:HALO + tm, :]
    for t in range(1, FFN_CONV):
        acc = acc + w[FFN_CONV - 1 - t:FFN_CONV - t, :] * ext_ref[HALO - t:HALO - t + tm, :]
    halo_ref[j] = ext_ref[tm:tm + HALO, :]
    gelu = 0.5 * acc * (1.0 + lax.erf(acc * (2.0 ** -0.5)))
    o_ref[...] = (gelu * _dot(an, wu_ref[...])).astype(o_ref.dtype)


def ffn_up(a, g, w_up, layer, conv_w, conv_b, *, tm=1024, tn=512):
    m, k = a.shape
    tm = min(tm, m)
    n_col = D_FF // tn
    kern = functools.partial(_ffn_up_kernel, tm=tm)
    return pl.pallas_call(
        kern,
        grid=(m // tm, n_col),
        in_specs=[
            pl.BlockSpec((tm, k), lambda i, j: (i, 0)),
            pl.BlockSpec((1, k), lambda i, j: (0, 0)),
            _wspec(k, tn, layer),
            _wspec(k, tn, layer, n_col),
            pl.BlockSpec((FFN_CONV, tn), lambda i, j: (0, j)),
            pl.BlockSpec((1, tn), lambda i, j: (0, j)),
        ],
        out_specs=pl.BlockSpec((tm, tn), lambda i, j: (i, j)),
        out_shape=jax.ShapeDtypeStruct((m, D_FF), BF16),
        scratch_shapes=[pltpu.VMEM((tm, k), BF16),
                        pltpu.VMEM((n_col, HALO, tn), F32),
                        pltpu.VMEM((tm + HALO, tn), F32)],
        compiler_params=_cparams(2),
        name="ffn_up",
    )(a, g.reshape(1, k), w_up, w_up, conv_w, conv_b.reshape(1, D_FF))


def _matmul_res_kernel(a_ref, b_ref, r_ref, o_ref):
    o_ref[...] = r_ref[...] + _dot(a_ref[...], b_ref[...])


def matmul_residual(a, w, layer, r, *, tm, tn):
    m, k = a.shape
    n = w.shape[2]
    tm = min(tm, m)
    tn = min(tn, n)
    return pl.pallas_call(
        _matmul_res_kernel,
        grid=(m // tm, n // tn),
        in_specs=[
            pl.BlockSpec((tm, k), lambda i, j: (i, 0)),
            _wspec(k, tn, layer),
            pl.BlockSpec((tm, tn), lambda i, j: (i, j)),
        ],
        out_specs=pl.BlockSpec((tm, tn), lambda i, j: (i, j)),
        out_shape=jax.ShapeDtypeStruct((m, n), F32),
        compiler_params=_cparams(2),
        name="matmul_residual",
    )(a, w, r)


def _matmul_res_norm_kernel(a_ref, b_ref, r_ref, g_ref, o_ref):
    o_ref[...] = _rms(r_ref[...] + _dot(a_ref[...], b_ref[...]), g_ref[...], NORM_EPS)


def matmul_residual_norm(a, w, layer, r, g, *, tm=512):
    m, k = a.shape
    n = w.shape[2]
    tm = min(tm, m)
    return pl.pallas_call(
        _matmul_res_norm_kernel,
        grid=(m // tm,),
        in_specs=[
            pl.BlockSpec((tm, k), lambda i: (i, 0)),
            pl.BlockSpec((None, k, n), lambda i: (layer, 0, 0), pipeline_mode=pl.Buffered(1)),
            pl.BlockSpec((tm, n), lambda i: (i, 0)),
            pl.BlockSpec((1, n), lambda i: (0, 0)),
        ],
        out_specs=pl.BlockSpec((tm, n), lambda i: (i, 0)),
        out_shape=jax.ShapeDtypeStruct((m, n), F32),
        compiler_params=_cparams(1),
        name="matmul_residual_norm",
    )(a, w, r, g.reshape(1, n))


def _merge_kernel(a1_ref, b1_ref, a2_ref, b2_ref, g1_ref, g2_ref, o_ref):
    y1 = _dot(a1_ref[...], b1_ref[...])
    y2 = _dot(a2_ref[...], b2_ref[...])
    o_ref[...] = (_sigmoid(g1_ref[...]) * y1 + _sigmoid(g2_ref[...]) * y2).astype(o_ref.dtype)


def merge_branches(o_hg, w_hg, o_m2, w_m2, layer, proj, *, tm=512, tn=1024):
    m, k = o_hg.shape
    n = w_hg.shape[2]
    tm = min(tm, m)
    g1 = OFF_GHG // tn
    g2 = OFF_GM2 // tn
    wspec = pl.BlockSpec((None, k, tn), lambda j, i: (layer, 0, j))
    return pl.pallas_call(
        _merge_kernel,
        grid=(n // tn, m // tm),
        in_specs=[
            pl.BlockSpec((tm, k), lambda j, i: (i, 0)),
            wspec,
            pl.BlockSpec((tm, k), lambda j, i: (i, 0)),
            wspec,
            pl.BlockSpec((tm, tn), lambda j, i: (i, g1 + j)),
            pl.BlockSpec((tm, tn), lambda j, i: (i, g2 + j)),
        ],
        out_specs=pl.BlockSpec((tm, tn), lambda j, i: (i, j)),
        out_shape=jax.ShapeDtypeStruct((m, n), BF16),
        compiler_params=_cparams(2),
        name="merge_branches",
    )(o_hg, w_hg, o_m2, w_m2, proj, proj)


def _hgrn2_kernel(q_ref, f_ref, i_ref, og_ref, lbl_ref, ng_ref, o_ref,
                  st_ref, sc_ref, stb_ref, bl_s, q_s, k_s, *, layer, tb):
    n_sub = tb // HG_SUB
    width = HG_HPS * HG_KDIM

    @pl.when(pl.program_id(1) == 0)
    def _():
        st_ref[...] = jnp.zeros_like(st_ref)

    logits = lbl_ref[...]
    e = jnp.exp(logits - jnp.max(logits, axis=0, keepdims=True))
    p = e / jnp.sum(e, axis=0, keepdims=True)
    lb = jnp.zeros((1, width), F32)
    for i in range(1, layer + 1):
        lb = lb + p[i:i + 1, :]
    lbf = jnp.maximum(lb, LB_FLOOR)

    z = f_ref[...]
    t = jnp.exp(-jnp.abs(z))
    r = 1.0 / (1.0 + t)
    pos = z >= 0.0
    log_f = jnp.log(jnp.where(pos, 1.0 + lbf * t, t + lbf) * r)
    kk = (1.0 - lb) * jnp.where(pos, t, 1.0) * r
    q = q_ref[...] * (HG_KDIM ** -0.5)
    v = i_ref[...].astype(BF16)

    bl = _seg_cumsum(log_f, HG_SUB)
    qt = (q * jnp.exp(bl)).astype(BF16)
    kinc = (kk * jnp.exp(-bl)).astype(BF16)
    worst = jnp.max(-bl)

    def blk(a, n, h):
        return a[n * HG_SUB:(n + 1) * HG_SUB, h * HG_KDIM:(h + 1) * HG_KDIM]

    for h in range(HG_HPS):
        for n in range(n_sub):
            sc_ref[h * n_sub + n] = _dot_nt(blk(qt, n, h), blk(kinc, n, h))

    for h in range(HG_HPS):
        st = st_ref[h]
        for n in range(n_sub):
            blc = blk(bl, n, h)
            b_last = blc[HG_SUB - 1:HG_SUB, :]
            kdec = (blk(kk, n, h) * jnp.exp(b_last - blc)).astype(BF16)
            stb_ref[h * n_sub + n] = st.astype(BF16)
            st = st * jnp.exp(b_last) + _dot_tn(blk(v, n, h), kdec)
        st_ref[h] = st

    col_id = lax.broadcasted_iota(jnp.int32, (HG_SUB, HG_SUB), 1)
    tri = lax.broadcasted_iota(jnp.int32, (HG_SUB, HG_SUB), 0) >= col_id

    @pl.when(worst > HG_SAFE_DECAY)
    def _():
        for h in range(HG_HPS):
            cols = slice(h * HG_KDIM, (h + 1) * HG_KDIM)
            bl_s[h * tb:(h + 1) * tb, :] = bl[:, cols]
            q_s[h * tb:(h + 1) * tb, :] = q[:, cols]
            k_s[h * tb:(h + 1) * tb, :] = kk[:, cols]

        def body(n, carry):
            off = pl.multiple_of(n * HG_SUB, HG_SUB)
            blc = bl_s[pl.ds(off, HG_SUB), :]
            qc = q_s[pl.ds(off, HG_SUB), :]
            kc = k_s[pl.ds(off, HG_SUB), :]
            sc = jnp.zeros((HG_SUB, HG_SUB), F32)
            for s in range(HG_SUB):
                d = jnp.exp(jnp.minimum(blc - blc[s:s + 1, :], 0.0))
                col = jnp.sum(qc * d * kc[s:s + 1, :], axis=-1, keepdims=True)
                sc = jnp.where(col_id == s, col, sc)
            sc_ref[n] = sc
            return carry

        lax.fori_loop(0, HG_HPS * n_sub, body, 0)

    for h in range(HG_HPS):
        cols = slice(h * HG_KDIM, (h + 1) * HG_KDIM)
        outs = []
        for n in range(n_sub):
            sc = jnp.where(tri, sc_ref[h * n_sub + n], 0.0).astype(BF16)
            outs.append(_dot(sc, blk(v, n, h)) + _dot_nt(blk(qt, n, h), stb_ref[h * n_sub + n]))
        o = jnp.concatenate(outs, axis=0) * _sigmoid(og_ref[:, cols])
        o_ref[:, cols] = _rms(o, ng_ref[:, cols], NORM_EPS).astype(o_ref.dtype)


def hgrn2_branch(proj, lb_logits, norm_g, layer, *, tb=512):
    s = proj.shape[0]
    tb = min(tb, s)
    depth = lb_logits.shape[0]
    n_sub = tb // HG_SUB
    width = HG_HPS * HG_KDIM
    kern = functools.partial(_hgrn2_kernel, layer=layer, tb=tb)

    def col(off):
        base = off // width
        return pl.BlockSpec((tb, width), lambda h, i: (i, base + h))

    return pl.pallas_call(
        kern,
        grid=(HG_HEADS // HG_HPS, s // tb),
        in_specs=[
            col(OFF_HQ), col(OFF_HF), col(OFF_HI), col(OFF_HOG),
            pl.BlockSpec((depth, width), lambda h, i: (0, h)),
            pl.BlockSpec((1, width), lambda h, i: (0, h)),
        ],
        out_specs=pl.BlockSpec((tb, width), lambda h, i: (i, h)),
        out_shape=jax.ShapeDtypeStruct((s, D_HG), BF16),
        scratch_shapes=[pltpu.VMEM((HG_HPS, HG_VDIM, HG_KDIM), F32),
                        pltpu.VMEM((HG_HPS * n_sub, HG_SUB, HG_SUB), F32),
                        pltpu.VMEM((HG_HPS * n_sub, HG_VDIM, HG_KDIM), BF16),
                        pltpu.VMEM((HG_HPS * tb, HG_KDIM), F32),
                        pltpu.VMEM((HG_HPS * tb, HG_KDIM), F32),
                        pltpu.VMEM((HG_HPS * tb, HG_KDIM), F32)],
        compiler_params=_cparams(2),
        name="hgrn2_branch",
    )(proj, proj, proj, proj, lb_logits, norm_g.reshape(1, D_HG))


def _mamba2_kernel(z_ref, x_ref, b_ref, c_ref, dt_ref, cum_ref, wx_ref, wb_ref, wc_ref,
                   bx_ref, bb_ref, bc_ref, dsk_ref, ng_ref,
                   o_ref, ext_ref, st_ref, stb_ref, y_ref, *, tb):
    g = pl.program_id(0)
    gw = M2_GROUP_W
    ns = M2_DSTATE
    w_all = gw + 2 * ns
    n_chunk = tb // CHUNK

    @pl.when(pl.program_id(1) == 0)
    def _():
        st_ref[...] = jnp.zeros_like(st_ref)
        ext_ref[0:HALO, :] = jnp.zeros((HALO, w_all), F32)

    ext_ref[HALO:HALO + tb, 0:gw] = x_ref[...]
    ext_ref[HALO:HALO + tb, gw:gw + ns] = b_ref[...]
    ext_ref[HALO:HALO + tb, gw + ns:w_all] = c_ref[...]
    w = jnp.concatenate([wx_ref[...], wb_ref[...], wc_ref[...]], axis=1)
    bias = jnp.concatenate([bx_ref[...], bb_ref[...], bc_ref[...]], axis=1)
    acc = bias + w[M2_CONV - 1:M2_CONV, :] * ext_ref[HALO:HALO + tb, :]
    for j in range(1, M2_CONV):
        acc = acc + w[M2_CONV - 1 - j:M2_CONV - j, :] * ext_ref[HALO - j:HALO - j + tb, :]
    ext_ref[0:HALO, :] = ext_ref[tb:tb + HALO, :]
    xbc = _silu(acc)
    xs = xbc[:, 0:gw]
    bm = xbc[:, gw:gw + ns].astype(BF16)
    cm = xbc[:, gw + ns:w_all].astype(BF16)

    r_id = lax.broadcasted_iota(jnp.int32, (DT_PAD, gw), 0)
    c_id = lax.broadcasted_iota(jnp.int32, (DT_PAD, gw), 1)
    expand = (r_id == g * M2_HPG + c_id // M2_HEADDIM).astype(BF16)
    d_parts = _split3(dt_ref[...])
    c_parts = _split3(cum_ref[...])
    dtx = _dot(d_parts[0], expand) + _dot(d_parts[1], expand) + _dot(d_parts[2], expand)
    cumx = _dot(c_parts[0], expand) + _dot(c_parts[1], expand) + _dot(c_parts[2], expand)
    xdt = xs * dtx
    ecum = jnp.exp(cumx)

    sr = lax.broadcasted_iota(jnp.int32, (HALO, DT_PAD), 0)
    sl = lax.broadcasted_iota(jnp.int32, (HALO, DT_PAD), 1)
    sel = jnp.logical_and(sr < M2_HPG, sl == g * M2_HPG + sr).astype(BF16)

    def rot(a):
        return jnp.concatenate([a[tb - CHUNK:, :], a[:tb - CHUNK, :]], axis=0)

    rows_all = _dot_nt(sel, c_parts[0]) + _dot_nt(sel, c_parts[1]) + _dot_nt(sel, c_parts[2])
    rows_rot = (_dot_nt(sel, rot(c_parts[0])) + _dot_nt(sel, rot(c_parts[1]))
                + _dot_nt(sel, rot(c_parts[2])))

    pair_w = 2 * CHUNK
    low_half = lax.broadcasted_iota(jnp.int32, (1, pair_w), 1) < CHUNK
    tri = (lax.broadcasted_iota(jnp.int32, (CHUNK, pair_w), 0)
           >= lax.broadcasted_iota(jnp.int32, (CHUNK, pair_w), 1) % CHUNK)
    head_of_col = lax.broadcasted_iota(jnp.int32, (1, gw), 1) // M2_HEADDIM

    def rows(a, n):
        return a[n * CHUNK:(n + 1) * CHUNK, :]

    for n in range(n_chunk):
        cb4 = _dot_nt(rows(cm, n), jnp.concatenate([rows(bm, n)] * M2_HPG, axis=0))
        cumc = rows(cumx, n)
        lparts = []
        for p_ in range(M2_HPG // 2):
            if n % 2 == 0:
                a0 = n * CHUNK
                rp = jnp.where(low_half, rows_all[2 * p_:2 * p_ + 1, a0:a0 + pair_w],
                               rows_rot[2 * p_ + 1:2 * p_ + 2, a0:a0 + pair_w])
            else:
                a_lo = ((n + 1) * CHUNK) % tb
                a_hi = (n - 1) * CHUNK
                rp = jnp.where(low_half, rows_rot[2 * p_:2 * p_ + 1, a_lo:a_lo + pair_w],
                               rows_all[2 * p_ + 1:2 * p_ + 2, a_hi:a_hi + pair_w])
            diff = cumc[:, p_ * pair_w:(p_ + 1) * pair_w] - rp
            lparts.append(jnp.where(tri, jnp.exp(jnp.where(tri, diff, 0.0)), 0.0))
        lhs = (cb4 * jnp.concatenate(lparts, axis=1)).astype(BF16)
        xdtb = rows(xdt, n).astype(BF16)
        rhs = jnp.concatenate(
            [jnp.where(head_of_col == h, xdtb, jnp.zeros_like(xdtb)) for h in range(M2_HPG)], axis=0)
        y_ref[n * CHUNK:(n + 1) * CHUNK, :] = _dot(lhs, rhs)

    st = st_ref[...]
    for n in range(n_chunk):
        cumc = rows(cumx, n)
        last = cumc[CHUNK - 1:CHUNK, :]
        wdec = (rows(xdt, n) * jnp.exp(last - cumc)).astype(BF16)
        stb_ref[n] = st.astype(BF16)
        st = st * jnp.exp(last) + _dot_tn(rows(bm, n), wdec)
    st_ref[...] = st

    for n in range(n_chunk):
        y_ref[n * CHUNK:(n + 1) * CHUNK, :] += _dot(rows(cm, n), stb_ref[n]) * rows(ecum, n)

    y = (y_ref[...] + dsk_ref[...] * xs) * _silu(z_ref[...])
    o_ref[...] = _rms(y, ng_ref[...], M2_NORM_EPS).astype(o_ref.dtype)


def mamba2_branch(proj, dt, cum, conv_w, conv_b, d_x, norm_g, *, tb=512):
    s = proj.shape[0]
    tb = min(tb, s)
    kern = functools.partial(_mamba2_kernel, tb=tb)
    gw = M2_GROUP_W
    ns = M2_DSTATE
    w_all = gw + 2 * ns

    def colw(off, width):
        base = off // width
        return pl.BlockSpec((tb, width), lambda g, i: (i, base + g))

    def par(rows, off, width):
        base = off // width
        return pl.BlockSpec((rows, width), lambda g, i: (0, base + g))

    conv_b = conv_b.reshape(1, -1)
    return pl.pallas_call(
        kern,
        grid=(M2_GROUPS, s // tb),
        in_specs=[
            colw(OFF_MZ, gw), colw(OFF_MX, gw), colw(OFF_MB, ns), colw(OFF_MC, ns),
            pl.BlockSpec((tb, DT_PAD), lambda g, i: (i, 0)),
            pl.BlockSpec((tb, DT_PAD), lambda g, i: (i, 0)),
            par(M2_CONV, 0, gw), par(M2_CONV, D_M2, ns), par(M2_CONV, D_M2 + D_BC, ns),
            par(1, 0, gw), par(1, D_M2, ns), par(1, D_M2 + D_BC, ns),
            par(1, 0, gw), par(1, 0, gw),
        ],
        out_specs=pl.BlockSpec((tb, gw), lambda g, i: (i, g)),
        out_shape=jax.ShapeDtypeStruct((s, D_M2), BF16),
        scratch_shapes=[pltpu.VMEM((tb + HALO, w_all), F32),
                        pltpu.VMEM((ns, gw), F32),
                        pltpu.VMEM((tb // CHUNK, ns, gw), BF16),
                        pltpu.VMEM((tb, gw), F32)],
        compiler_params=_cparams(2),
        name="mamba2_branch",
    )(proj, proj, proj, proj, dt, cum, conv_w, conv_w, conv_w, conv_b, conv_b, conv_b,
      d_x, norm_g.reshape(1, D_M2))


def _xattn_kernel(q_ref, k_ref, v_ref, o_ref):
    scale = XA_HEADDIM ** -0.5
    for h in range(XA_HEADS):
        lo = h * XA_HEADDIM
        hi = lo + XA_HEADDIM
        sc = _dot_nt(q_ref[:, lo:hi], k_ref[:, lo:hi]) * scale
        e = jnp.exp(sc - jnp.max(sc, axis=-1, keepdims=True))
        p = e / jnp.sum(e, axis=-1, keepdims=True)
        o_ref[:, lo:hi] = _dot(p.astype(BF16), v_ref[:, lo:hi]).astype(o_ref.dtype)


def cross_attention(q, kv, *, tm=512):
    s = q.shape[0]
    n_mem = kv.shape[0]
    tm = min(tm, s)
    return pl.pallas_call(
        _xattn_kernel,
        grid=(s // tm,),
        in_specs=[
            pl.BlockSpec((tm, D_MODEL), lambda i: (i, 0)),
            pl.BlockSpec((n_mem, D_MODEL), lambda i: (0, 0)),
            pl.BlockSpec((n_mem, D_MODEL), lambda i: (0, 1)),
        ],
        out_specs=pl.BlockSpec((tm, D_MODEL), lambda i: (i, 0)),
        out_shape=jax.ShapeDtypeStruct((s, D_MODEL), BF16),
        compiler_params=_cparams(1),
        name="cross_attention",
    )(q, kv, kv)


def _per_head_lanes(p):
    return jnp.repeat(p, M2_HEADDIM).reshape(1, D_M2)


def kernel(x, mem, mix_norm_g, w_in, hg_lb_logits, hg_norm_g, m2_conv_w, m2_conv_b, m2_dt_bias, m2_A_log, m2_D, m2_norm_g, w_branch_hg, w_branch_m2, w_out, mem_norm_g, xa_norm_g, xa_wq, xa_wkv, xa_wo, ffn_norm_g, ffn_w_up, ffn_conv_w, ffn_conv_b, ffn_w_down, final_norm_g):
    bsz, seq, d = x.shape
    depth = w_in.shape[0]
    w_proj = jnp.concatenate([w_in[:, :, :IN_DT], w_in[:, :, IN_GHG:]], axis=2).astype(BF16)
    w_dt = w_in[:, :, IN_DT:IN_DT + DT_PAD].astype(BF16)
    w_bhg = w_branch_hg.astype(BF16)
    w_bm2 = w_branch_m2.astype(BF16)
    w_o = w_out.astype(BF16)
    wq = xa_wq.astype(BF16)
    wkv = xa_wkv.astype(BF16)
    wo = xa_wo.astype(BF16)
    w_up = ffn_w_up.astype(BF16)
    w_down = ffn_w_down.astype(BF16)
    outs = []
    for b in range(bsz):
        h = x[b]
        mem_b = mem[b]
        for l in range(depth):
            proj, dt, cum = in_proj(h, mix_norm_g[l], w_proj, w_dt, l, m2_dt_bias[l], m2_A_log[l])
            o_hg = hgrn2_branch(proj, hg_lb_logits, hg_norm_g[l], l)
            o_m2 = mamba2_branch(proj, dt, cum, m2_conv_w[l], m2_conv_b[l],
                                 _per_head_lanes(m2_D[l]), m2_norm_g[l])
            merged = merge_branches(o_hg, w_bhg, o_m2, w_bm2, l, proj)
            h = matmul_residual(merged, w_o, l, h, tm=512, tn=D_MODEL)

            q = norm_matmul(h, xa_norm_g[l], wq, l, out_dtype=BF16)
            kv = norm_matmul(mem_b, mem_norm_g, wkv, l, out_dtype=BF16)
            att = cross_attention(q, kv)
            h = matmul_residual(att, wo, l, h, tm=512, tn=D_MODEL)

            act = ffn_up(h, ffn_norm_g[l], w_up, l, ffn_conv_w[l], ffn_conv_b[l])
            if l + 1 < depth:
                h = matmul_residual(act, w_down, l, h, tm=1024, tn=512)
            else:
                h = matmul_residual_norm(act, w_down, l, h, final_norm_g)
        outs.append(h)
    return jnp.stack(outs, axis=0)
```

```python
import functools

import jax
import jax.numpy as jnp
from jax import lax
from jax.experimental import pallas as pl
from jax.experimental.pallas import tpu as pltpu

F32 = jnp.float32
BF16 = jnp.bfloat16

D_MODEL = 2048
HG_HEADS = 16
HG_KDIM = 128
HG_VDIM = 128
D_HG = HG_HEADS * HG_KDIM
M2_HEADS = 32
M2_HEADDIM = 64
D_M2 = M2_HEADS * M2_HEADDIM
M2_GROUPS = 8
M2_DSTATE = 128
M2_CONV = 4
M2_GROUP_W = D_M2 // M2_GROUPS
M2_HPG = M2_HEADS // M2_GROUPS
D_BC = M2_GROUPS * M2_DSTATE
CHUNK = 64
HG_SUB = 32
HG_HPS = 4
XA_HEADS = 4
XA_HEADDIM = D_MODEL // XA_HEADS
D_FF = 5632
FFN_CONV = 3
NORM_EPS = 1e-6
M2_NORM_EPS = 1e-5
LB_FLOOR = 1e-30
HG_SAFE_DECAY = 80.0

IN_DT = 4 * D_HG + 2 * D_M2 + 2 * D_BC
IN_GHG = IN_DT + M2_HEADS
OFF_HQ = 0
OFF_HF = OFF_HQ + D_HG
OFF_HI = OFF_HF + D_HG
OFF_HOG = OFF_HI + D_HG
OFF_MZ = OFF_HOG + D_HG
OFF_MX = OFF_MZ + D_M2
OFF_MB = OFF_MX + D_M2
OFF_MC = OFF_MB + D_BC
OFF_GHG = OFF_MC + D_BC
OFF_GM2 = OFF_GHG + D_MODEL
DT_PAD = 128

HALO = 8
VMEM_LIMIT = 56 * 1024 * 1024


def _cparams(n_axes):
    return pltpu.CompilerParams(
        dimension_semantics=("arbitrary",) * n_axes,
        vmem_limit_bytes=VMEM_LIMIT)


def _rms(x, g, eps):
    return x * lax.rsqrt(jnp.mean(x * x, axis=-1, keepdims=True) + eps) * g


def _sigmoid(x):
    return 0.5 * jnp.tanh(0.5 * x) + 0.5


def _silu(x):
    hx = 0.5 * x
    return hx * jnp.tanh(hx) + hx


def _softplus(x):
    return jnp.maximum(x, 0.0) + jnp.log1p(jnp.exp(-jnp.abs(x)))


def _dot(a, b):
    return jnp.dot(a, b, preferred_element_type=F32)


def _dot_nt(a, b):
    return lax.dot_general(a, b, (((1,), (1,)), ((), ())), preferred_element_type=F32)


def _dot_tn(a, b):
    return lax.dot_general(a, b, (((0,), (0,)), ((), ())), preferred_element_type=F32)


def _split3(x):
    hi = x.astype(BF16)
    r1 = x - hi.astype(F32)
    mid = r1.astype(BF16)
    lo = (r1 - mid.astype(F32)).astype(BF16)
    return hi, mid, lo


def _seg_cumsum(x, seg):
    row = lax.broadcasted_iota(jnp.int32, x.shape, 0) % seg
    k = 1
    while k < seg:
        x = x + jnp.where(row >= k, pltpu.roll(x, k, axis=0), 0.0)
        k *= 2
    return x


def _wspec(k, tn, layer, col0=0):
    return pl.BlockSpec((None, k, tn), lambda i, j: (layer, 0, col0 + j))


def _norm_matmul_kernel(a_ref, g_ref, b_ref, o_ref, an_ref):
    @pl.when(pl.program_id(1) == 0)
    def _():
        an_ref[...] = _rms(a_ref[...], g_ref[...], NORM_EPS).astype(BF16)

    o_ref[...] = _dot(an_ref[...], b_ref[...]).astype(o_ref.dtype)


def norm_matmul(a, g, w, layer, *, out_dtype, tm=1024, tn=1024):
    m, k = a.shape
    n = w.shape[2]
    tm = min(tm, m)
    tn = min(tn, n)
    return pl.pallas_call(
        _norm_matmul_kernel,
        grid=(m // tm, n // tn),
        in_specs=[
            pl.BlockSpec((tm, k), lambda i, j: (i, 0)),
            pl.BlockSpec((1, k), lambda i, j: (0, 0)),
            _wspec(k, tn, layer),
        ],
        out_specs=pl.BlockSpec((tm, tn), lambda i, j: (i, j)),
        out_shape=jax.ShapeDtypeStruct((m, n), out_dtype),
        scratch_shapes=[pltpu.VMEM((tm, k), BF16)],
        compiler_params=_cparams(2),
        name="norm_matmul",
    )(a, g.reshape(1, k), w)


def _in_proj_kernel(a_ref, g_ref, b_ref, wdt_ref, bias_ref, alog_ref, o_ref, dt_ref, cum_ref, an_ref):
    @pl.when(pl.program_id(1) == 0)
    def _():
        an = _rms(a_ref[...], g_ref[...], NORM_EPS).astype(BF16)
        an_ref[...] = an
        dt = _softplus(_dot(an, wdt_ref[...]) + bias_ref[...])
        dt_ref[...] = dt
        cum_ref[...] = _seg_cumsum(dt * (-jnp.exp(alog_ref[...])), CHUNK)

    o_ref[...] = _dot(an_ref[...], b_ref[...])


def in_proj(a, g, w, w_dt, layer, dt_bias, a_log, *, tm=1024, tn=1536):
    m, k = a.shape
    n = w.shape[2]
    tm = min(tm, m)

    def pad(p):
        return jnp.pad(p, (0, DT_PAD - M2_HEADS)).reshape(1, DT_PAD)

    small = jax.ShapeDtypeStruct((m, DT_PAD), F32)
    small_spec = pl.BlockSpec((tm, DT_PAD), lambda i, j: (i, 0))
    return pl.pallas_call(
        _in_proj_kernel,
        grid=(m // tm, n // tn),
        in_specs=[
            pl.BlockSpec((tm, k), lambda i, j: (i, 0)),
            pl.BlockSpec((1, k), lambda i, j: (0, 0)),
            _wspec(k, tn, layer),
            pl.BlockSpec((None, k, DT_PAD), lambda i, j: (layer, 0, 0)),
            pl.BlockSpec((1, DT_PAD), lambda i, j: (0, 0)),
            pl.BlockSpec((1, DT_PAD), lambda i, j: (0, 0)),
        ],
        out_specs=[pl.BlockSpec((tm, tn), lambda i, j: (i, j)), small_spec, small_spec],
        out_shape=[jax.ShapeDtypeStruct((m, n), F32), small, small],
        scratch_shapes=[pltpu.VMEM((tm, k), BF16)],
        compiler_params=_cparams(2),
        name="in_proj",
    )(a, g.reshape(1, k), w, w_dt, pad(dt_bias), pad(a_log))


def _ffn_up_kernel(a_ref, g_ref, wg_ref, wu_ref, cw_ref, cb_ref, o_ref, an_ref, halo_ref, ext_ref,
                   *, tm):
    i = pl.program_id(0)
    j = pl.program_id(1)

    @pl.when(jnp.logical_and(i == 0, j == 0))
    def _():
        halo_ref[...] = jnp.zeros_like(halo_ref)

    @pl.when(j == 0)
    def _():
        an_ref[...] = _rms(a_ref[...], g_ref[...], NORM_EPS).astype(BF16)

    an = an_ref[...]
    ext_ref[0:HALO, :] = halo_ref[j]
    ext_ref[HALO:HALO + tm, :] = _dot(an, wg_ref[...])
    w = cw_ref[...]
    acc = cb_ref[...] + w[FFN_CONV - 1:FFN_CONV, :] * ext_ref[HALO:HALO + tm, :]
    for t in range(1, FFN_CONV):
        acc = acc + w[FFN_CONV - 1 - t:FFN_CONV - t, :] * ext_ref[HALO - t:HALO - t + tm, :]
    halo_ref[j] = ext_ref[tm:tm + HALO, :]
    gelu = 0.5 * acc * (1.0 + lax.erf(acc * (2.0 ** -0.5)))
    o_ref[...] = (gelu * _dot(an, wu_ref[...])).astype(o_ref.dtype)


def ffn_up(a, g, w_up, layer, conv_w, conv_b, *, tm=1024, tn=512):
    m, k = a.shape
    tm = min(tm, m)
    n_col = D_FF // tn
    kern = functools.partial(_ffn_up_kernel, tm=tm)
    return pl.pallas_call(
        kern,
        grid=(m // tm, n_col),
        in_specs=[
            pl.BlockSpec((tm, k), lambda i, j: (i, 0)),
            pl.BlockSpec((1, k), lambda i, j: (0, 0)),
            _wspec(k, tn, layer),
            _wspec(k, tn, layer, n_col),
            pl.BlockSpec((FFN_CONV, tn), lambda i, j: (0, j)),
            pl.BlockSpec((1, tn), lambda i, j: (0, j)),
        ],
        out_specs=pl.BlockSpec((tm, tn), lambda i, j: (i, j)),
        out_shape=jax.ShapeDtypeStruct((m, D_FF), BF16),
        scratch_shapes=[pltpu.VMEM((tm, k), BF16),
                        pltpu.VMEM((n_col, HALO, tn), F32),
                        pltpu.VMEM((tm + HALO, tn), F32)],
        compiler_params=_cparams(2),
        name="ffn_up",
    )(a, g.reshape(1, k), w_up, w_up, conv_w, conv_b.reshape(1, D_FF))


def _matmul_res_kernel(a_ref, b_ref, r_ref, o_ref):
    o_ref[...] = r_ref[...] + _dot(a_ref[...], b_ref[...])


def matmul_residual(a, w, layer, r, *, tm, tn):
    m, k = a.shape
    n = w.shape[2]
    tm = min(tm, m)
    tn = min(tn, n)
    return pl.pallas_call(
        _matmul_res_kernel,
        grid=(m // tm, n // tn),
        in_specs=[
            pl.BlockSpec((tm, k), lambda i, j: (i, 0)),
            _wspec(k, tn, layer),
            pl.BlockSpec((tm, tn), lambda i, j: (i, j)),
        ],
        out_specs=pl.BlockSpec((tm, tn), lambda i, j: (i, j)),
        out_shape=jax.ShapeDtypeStruct((m, n), F32),
        compiler_params=_cparams(2),
        name="matmul_residual",
    )(a, w, r)


def _matmul_res_norm_kernel(a_ref, b_ref, r_ref, g_ref, o_ref):
    o_ref[...] = _rms(r_ref[...] + _dot(a_ref[...], b_ref[...]), g_ref[...], NORM_EPS)


def matmul_residual_norm(a, w, layer, r, g, *, tm=512):
    m, k = a.shape
    n = w.shape[2]
    tm = min(tm, m)
    return pl.pallas_call(
        _matmul_res_norm_kernel,
        grid=(m // tm,),
        in_specs=[
            pl.BlockSpec((tm, k), lambda i: (i, 0)),
            pl.BlockSpec((None, k, n), lambda i: (layer, 0, 0), pipeline_mode=pl.Buffered(1)),
            pl.BlockSpec((tm, n), lambda i: (i, 0)),
            pl.BlockSpec((1, n), lambda i: (0, 0)),
        ],
        out_specs=pl.BlockSpec((tm, n), lambda i: (i, 0)),
        out_shape=jax.ShapeDtypeStruct((m, n), F32),
        compiler_params=_cparams(1),
        name="matmul_residual_norm",
    )(a, w, r, g.reshape(1, n))


def _merge_kernel(a1_ref, b1_ref, a2_ref, b2_ref, g1_ref, g2_ref, o_ref):
    y1 = _dot(a1_ref[...], b1_ref[...])
    y2 = _dot(a2_ref[...], b2_ref[...])
    o_ref[...] = (_sigmoid(g1_ref[...]) * y1 + _sigmoid(g2_ref[...]) * y2).astype(o_ref.dtype)


def merge_branches(o_hg, w_hg, o_m2, w_m2, layer, proj, *, tm=512, tn=1024):
    m, k = o_hg.shape
    n = w_hg.shape[2]
    tm = min(tm, m)
    g1 = OFF_GHG // tn
    g2 = OFF_GM2 // tn
    wspec = pl.BlockSpec((None, k, tn), lambda j, i: (layer, 0, j))
    return pl.pallas_call(
        _merge_kernel,
        grid=(n // tn, m // tm),
        in_specs=[
            pl.BlockSpec((tm, k), lambda j, i: (i, 0)),
            wspec,
            pl.BlockSpec((tm, k), lambda j, i: (i, 0)),
            wspec,
            pl.BlockSpec((tm, tn), lambda j, i: (i, g1 + j)),
            pl.BlockSpec((tm, tn), lambda j, i: (i, g2 + j)),
        ],
        out_specs=pl.BlockSpec((tm, tn), lambda j, i: (i, j)),
        out_shape=jax.ShapeDtypeStruct((m, n), BF16),
        compiler_params=_cparams(2),
        name="merge_branches",
    )(o_hg, w_hg, o_m2, w_m2, proj, proj)


def _hgrn2_kernel(q_ref, f_ref, i_ref, og_ref, lbl_ref, ng_ref, o_ref,
                  st_ref, sc_ref, stb_ref, bl_s, q_s, k_s, *, layer, tb):
    n_sub = tb // HG_SUB
    width = HG_HPS * HG_KDIM

    @pl.when(pl.program_id(1) == 0)
    def _():
        st_ref[...] = jnp.zeros_like(st_ref)

    logits = lbl_ref[...]
    e = jnp.exp(logits - jnp.max(logits, axis=0, keepdims=True))
    p = e / jnp.sum(e, axis=0, keepdims=True)
    lb = jnp.zeros((1, width), F32)
    for i in range(1, layer + 1):
        lb = lb + p[i:i + 1, :]
    lbf = jnp.maximum(lb, LB_FLOOR)

    z = f_ref[...]
    t = jnp.exp(-jnp.abs(z))
    r = 1.0 / (1.0 + t)
    pos = z >= 0.0
    log_f = jnp.log(jnp.where(pos, 1.0 + lbf * t, t + lbf) * r)
    kk = (1.0 - lb) * jnp.where(pos, t, 1.0) * r
    q = q_ref[...] * (HG_KDIM ** -0.5)
    v = i_ref[...].astype(BF16)

    bl = _seg_cumsum(log_f, HG_SUB)
    qt = (q * jnp.exp(bl)).astype(BF16)
    kinc = (kk * jnp.exp(-bl)).astype(BF16)
    worst = jnp.max(-bl)

    def blk(a, n, h):
        return a[n * HG_SUB:(n + 1) * HG_SUB, h * HG_KDIM:(h + 1) * HG_KDIM]

    for h in range(HG_HPS):
        for n in range(n_sub):
            sc_ref[h * n_sub + n] = _dot_nt(blk(qt, n, h), blk(kinc, n, h))

    for h in range(HG_HPS):
        st = st_ref[h]
        for n in range(n_sub):
            blc = blk(bl, n, h)
            b_last = blc[HG_SUB - 1:HG_SUB, :]
            kdec = (blk(kk, n, h) * jnp.exp(b_last - blc)).astype(BF16)
            stb_ref[h * n_sub + n] = st.astype(BF16)
            st = st * jnp.exp(b_last) + _dot_tn(blk(v, n, h), kdec)
        st_ref[h] = st

    col_id = lax.broadcasted_iota(jnp.int32, (HG_SUB, HG_SUB), 1)
    tri = lax.broadcasted_iota(jnp.int32, (HG_SUB, HG_SUB), 0) >= col_id

    @pl.when(worst > HG_SAFE_DECAY)
    def _():
        for h in range(HG_HPS):
            cols = slice(h * HG_KDIM, (h + 1) * HG_KDIM)
            bl_s[h * tb:(h + 1) * tb, :] = bl[:, cols]
            q_s[h * tb:(h + 1) * tb, :] = q[:, cols]
            k_s[h * tb:(h + 1) * tb, :] = kk[:, cols]

        def body(n, carry):
            off = pl.multiple_of(n * HG_SUB, HG_SUB)
            blc = bl_s[pl.ds(off, HG_SUB), :]
            qc = q_s[pl.ds(off, HG_SUB), :]
            kc = k_s[pl.ds(off, HG_SUB), :]
            sc = jnp.zeros((HG_SUB, HG_SUB), F32)
            for s in range(HG_SUB):
                d = jnp.exp(jnp.minimum(blc - blc[s:s + 1, :], 0.0))
                col = jnp.sum(qc * d * kc[s:s + 1, :], axis=-1, keepdims=True)
                sc = jnp.where(col_id == s, col, sc)
            sc_ref[n] = sc
            return carry

        lax.fori_loop(0, HG_HPS * n_sub, body, 0)

    for h in range(HG_HPS):
        cols = slice(h * HG_KDIM, (h + 1) * HG_KDIM)
        outs = []
        for n in range(n_sub):
            sc = jnp.where(tri, sc_ref[h * n_sub + n], 0.0).astype(BF16)
            outs.append(_dot(sc, blk(v, n, h)) + _dot_nt(blk(qt, n, h), stb_ref[h * n_sub + n]))
        o = jnp.concatenate(outs, axis=0) * _sigmoid(og_ref[:, cols])
        o_ref[:, cols] = _rms(o, ng_ref[:, cols], NORM_EPS).astype(o_ref.dtype)


def hgrn2_branch(proj, lb_logits, norm_g, layer, *, tb=1024):
    s = proj.shape[0]
    tb = min(tb, s)
    depth = lb_logits.shape[0]
    n_sub = tb // HG_SUB
    width = HG_HPS * HG_KDIM
    kern = functools.partial(_hgrn2_kernel, layer=layer, tb=tb)

    def col(off):
        base = off // width
        return pl.BlockSpec((tb, width), lambda h, i: (i, base + h))

    return pl.pallas_call(
        kern,
        grid=(HG_HEADS // HG_HPS, s // tb),
        in_specs=[
            col(OFF_HQ), col(OFF_HF), col(OFF_HI), col(OFF_HOG),
            pl.BlockSpec((depth, width), lambda h, i: (0, h)),
            pl.BlockSpec((1, width), lambda h, i: (0, h)),
        ],
        out_specs=pl.BlockSpec((tb, width), lambda h, i: (i, h)),
        out_shape=jax.ShapeDtypeStruct((s, D_HG), BF16),
        scratch_shapes=[pltpu.VMEM((HG_HPS, HG_VDIM, HG_KDIM), F32),
                        pltpu.VMEM((HG_HPS * n_sub, HG_SUB, HG_SUB), F32),
                        pltpu.VMEM((HG_HPS * n_sub, HG_VDIM, HG_KDIM), BF16),
                        pltpu.VMEM((HG_HPS * tb, HG_KDIM), F32),
                        pltpu.VMEM((HG_HPS * tb, HG_KDIM), F32),
                        pltpu.VMEM((HG_HPS * tb, HG_KDIM), F32)],
        compiler_params=_cparams(2),
        name="hgrn2_branch",
    )(proj, proj, proj, proj, lb_logits, norm_g.reshape(1, D_HG))


def _mamba2_kernel(z_ref, x_ref, b_ref, c_ref, dt_ref, cum_ref, wx_ref, wb_ref, wc_ref,
                   bx_ref, bb_ref, bc_ref, dsk_ref, ng_ref,
                   o_ref, ext_ref, st_ref, stb_ref, y_ref, *, tb):
    g = pl.program_id(0)
    gw = M2_GROUP_W
    ns = M2_DSTATE
    w_all = gw + 2 * ns
    n_chunk = tb // CHUNK

    @pl.when(pl.program_id(1) == 0)
    def _():
        st_ref[...] = jnp.zeros_like(st_ref)
        ext_ref[0:HALO, :] = jnp.zeros((HALO, w_all), F32)

    ext_ref[HALO:HALO + tb, 0:gw] = x_ref[...]
    ext_ref[HALO:HALO + tb, gw:gw + ns] = b_ref[...]
    ext_ref[HALO:HALO + tb, gw + ns:w_all] = c_ref[...]
    w = jnp.concatenate([wx_ref[...], wb_ref[...], wc_ref[...]], axis=1)
    bias = jnp.concatenate([bx_ref[...], bb_ref[...], bc_ref[...]], axis=1)
    acc = bias + w[M2_CONV - 1:M2_CONV, :] * ext_ref[HALO:HALO + tb, :]
    for j in range(1, M2_CONV):
        acc = acc + w[M2_CONV - 1 - j:M2_CONV - j, :] * ext_ref[HALO - j:HALO - j + tb, :]
    ext_ref[0:HALO, :] = ext_ref[tb:tb + HALO, :]
    xbc = _silu(acc)
    xs = xbc[:, 0:gw]
    bm = xbc[:, gw:gw + ns].astype(BF16)
    cm = xbc[:, gw + ns:w_all].astype(BF16)

    r_id = lax.broadcasted_iota(jnp.int32, (DT_PAD, gw), 0)
    c_id = lax.broadcasted_iota(jnp.int32, (DT_PAD, gw), 1)
    expand = (r_id == g * M2_HPG + c_id // M2_HEADDIM).astype(BF16)
    d_parts = _split3(dt_ref[...])
    c_parts = _split3(cum_ref[...])
    dtx = _dot(d_parts[0], expand) + _dot(d_parts[1], expand) + _dot(d_parts[2], expand)
    cumx = _dot(c_parts[0], expand) + _dot(c_parts[1], expand) + _dot(c_parts[2], expand)
    xdt = xs * dtx
    ecum = jnp.exp(cumx)

    sr = lax.broadcasted_iota(jnp.int32, (HALO, DT_PAD), 0)
    sl = lax.broadcasted_iota(jnp.int32, (HALO, DT_PAD), 1)
    sel = jnp.logical_and(sr < M2_HPG, sl == g * M2_HPG + sr).astype(BF16)

    def rot(a):
        return jnp.concatenate([a[tb - CHUNK:, :], a[:tb - CHUNK, :]], axis=0)

    rows_all = _dot_nt(sel, c_parts[0]) + _dot_nt(sel, c_parts[1]) + _dot_nt(sel, c_parts[2])
    rows_rot = (_dot_nt(sel, rot(c_parts[0])) + _dot_nt(sel, rot(c_parts[1]))
                + _dot_nt(sel, rot(c_parts[2])))

    pair_w = 2 * CHUNK
    low_half = lax.broadcasted_iota(jnp.int32, (1, pair_w), 1) < CHUNK
    tri = (lax.broadcasted_iota(jnp.int32, (CHUNK, pair_w), 0)
           >= lax.broadcasted_iota(jnp.int32, (CHUNK, pair_w), 1) % CHUNK)
    head_of_col = lax.broadcasted_iota(jnp.int32, (1, gw), 1) // M2_HEADDIM

    def rows(a, n):
        return a[n * CHUNK:(n + 1) * CHUNK, :]

    for n in range(n_chunk):
        cb4 = _dot_nt(rows(cm, n), jnp.concatenate([rows(bm, n)] * M2_HPG, axis=0))
        cumc = rows(cumx, n)
        lparts = []
        for p_ in range(M2_HPG // 2):
            if n % 2 == 0:
                a0 = n * CHUNK
                rp = jnp.where(low_half, rows_all[2 * p_:2 * p_ + 1, a0:a0 + pair_w],
                               rows_rot[2 * p_ + 1:2 * p_ + 2, a0:a0 + pair_w])
            else:
                a_lo = ((n + 1) * CHUNK) % tb
                a_hi = (n - 1) * CHUNK
                rp = jnp.where(low_half, rows_rot[2 * p_:2 * p_ + 1, a_lo:a_lo + pair_w],
                               rows_all[2 * p_ + 1:2 * p_ + 2, a_hi:a_hi + pair_w])
            diff = cumc[:, p_ * pair_w:(p_ + 1) * pair_w] - rp
            lparts.append(jnp.where(tri, jnp.exp(jnp.where(tri, diff, 0.0)), 0.0))
        lhs = (cb4 * jnp.concatenate(lparts, axis=1)).astype(BF16)
        xdtb = rows(xdt, n).astype(BF16)
        rhs = jnp.concatenate(
            [jnp.where(head_of_col == h, xdtb, jnp.zeros_like(xdtb)) for h in range(M2_HPG)], axis=0)
        y_ref[n * CHUNK:(n + 1) * CHUNK, :] = _dot(lhs, rhs)

    st = st_ref[...]
    for n in range(n_chunk):
        cumc = rows(cumx, n)
        last = cumc[CHUNK - 1:CHUNK, :]
        wdec = (rows(xdt, n) * jnp.exp(last - cumc)).astype(BF16)
        stb_ref[n] = st.astype(BF16)
        st = st * jnp.exp(last) + _dot_tn(rows(bm, n), wdec)
    st_ref[...] = st

    for n in range(n_chunk):
        y_ref[n * CHUNK:(n + 1) * CHUNK, :] += _dot(rows(cm, n), stb_ref[n]) * rows(ecum, n)

    y = (y_ref[...] + dsk_ref[...] * xs) * _silu(z_ref[...])
    o_ref[...] = _rms(y, ng_ref[...], M2_NORM_EPS).astype(o_ref.dtype)


def mamba2_branch(proj, dt, cum, conv_w, conv_b, d_x, norm_g, *, tb=1024):
    s = proj.shape[0]
    tb = min(tb, s)
    kern = functools.partial(_mamba2_kernel, tb=tb)
    gw = M2_GROUP_W
    ns = M2_DSTATE
    w_all = gw + 2 * ns

    def colw(off, width):
        base = off // width
        return pl.BlockSpec((tb, width), lambda g, i: (i, base + g))

    def par(rows, off, width):
        base = off // width
        return pl.BlockSpec((rows, width), lambda g, i: (0, base + g))

    conv_b = conv_b.reshape(1, -1)
    return pl.pallas_call(
        kern,
        grid=(M2_GROUPS, s // tb),
        in_specs=[
            colw(OFF_MZ, gw), colw(OFF_MX, gw), colw(OFF_MB, ns), colw(OFF_MC, ns),
            pl.BlockSpec((tb, DT_PAD), lambda g, i: (i, 0)),
            pl.BlockSpec((tb, DT_PAD), lambda g, i: (i, 0)),
            par(M2_CONV, 0, gw), par(M2_CONV, D_M2, ns), par(M2_CONV, D_M2 + D_BC, ns),
            par(1, 0, gw), par(1, D_M2, ns), par(1, D_M2 + D_BC, ns),
            par(1, 0, gw), par(1, 0, gw),
        ],
        out_specs=pl.BlockSpec((tb, gw), lambda g, i: (i, g)),
        out_shape=jax.ShapeDtypeStruct((s, D_M2), BF16),
        scratch_shapes=[pltpu.VMEM((tb + HALO, w_all), F32),
                        pltpu.VMEM((ns, gw), F32),
                        pltpu.VMEM((tb // CHUNK, ns, gw), BF16),
                        pltpu.VMEM((tb, gw), F32)],
        compiler_params=_cparams(2),
        name="mamba2_branch",
    )(proj, proj, proj, proj, dt, cum, conv_w, conv_w, conv_w, conv_b, conv_b, conv_b,
      d_x, norm_g.reshape(1, D_M2))


def _xattn_kernel(q_ref, k_ref, v_ref, o_ref):
    scale = XA_HEADDIM ** -0.5
    for h in range(XA_HEADS):
        lo = h * XA_HEADDIM
        hi = lo + XA_HEADDIM
        sc = _dot_nt(q_ref[:, lo:hi], k_ref[:, lo:hi]) * scale
        e = jnp.exp(sc - jnp.max(sc, axis=-1, keepdims=True))
        p = e / jnp.sum(e, axis=-1, keepdims=True)
        o_ref[:, lo:hi] = _dot(p.astype(BF16), v_ref[:, lo:hi]).astype(o_ref.dtype)


def cross_attention(q, kv, *, tm=512):
    s = q.shape[0]
    n_mem = kv.shape[0]
    tm = min(tm, s)
    return pl.pallas_call(
        _xattn_kernel,
        grid=(s // tm,),
        in_specs=[
            pl.BlockSpec((tm, D_MODEL), lambda i: (i, 0)),
            pl.BlockSpec((n_mem, D_MODEL), lambda i: (0, 0)),
            pl.BlockSpec((n_mem, D_MODEL), lambda i: (0, 1)),
        ],
        out_specs=pl.BlockSpec((tm, D_MODEL), lambda i: (i, 0)),
        out_shape=jax.ShapeDtypeStruct((s, D_MODEL), BF16),
        compiler_params=_cparams(1),
        name="cross_attention",
    )(q, kv, kv)


def _per_head_lanes(p):
    return jnp.repeat(p, M2_HEADDIM).reshape(1, D_M2)


def kernel(x, mem, mix_norm_g, w_in, hg_lb_logits, hg_norm_g, m2_conv_w, m2_conv_b, m2_dt_bias, m2_A_log, m2_D, m2_norm_g, w_branch_hg, w_branch_m2, w_out, mem_norm_g, xa_norm_g, xa_wq, xa_wkv, xa_wo, ffn_norm_g, ffn_w_up, ffn_conv_w, ffn_conv_b, ffn_w_down, final_norm_g):
    bsz, seq, d = x.shape
    depth = w_in.shape[0]
    w_proj = jnp.concatenate([w_in[:, :, :IN_DT], w_in[:, :, IN_GHG:]], axis=2).astype(BF16)
    w_dt = w_in[:, :, IN_DT:IN_DT + DT_PAD].astype(BF16)
    w_bhg = w_branch_hg.astype(BF16)
    w_bm2 = w_branch_m2.astype(BF16)
    w_o = w_out.astype(BF16)
    wq = xa_wq.astype(BF16)
    wkv = xa_wkv.astype(BF16)
    wo = xa_wo.astype(BF16)
    w_up = ffn_w_up.astype(BF16)
    w_down = ffn_w_down.astype(BF16)
    outs = []
    for b in range(bsz):
        h = x[b]
        mem_b = mem[b]
        for l in range(depth):
            proj, dt, cum = in_proj(h, mix_norm_g[l], w_proj, w_dt, l, m2_dt_bias[l], m2_A_log[l])
            o_hg = hgrn2_branch(proj, hg_lb_logits, hg_norm_g[l], l)
            o_m2 = mamba2_branch(proj, dt, cum, m2_conv_w[l], m2_conv_b[l],
                                 _per_head_lanes(m2_D[l]), m2_norm_g[l])
            merged = merge_branches(o_hg, w_bhg, o_m2, w_bm2, l, proj)
            h = matmul_residual(merged, w_o, l, h, tm=512, tn=D_MODEL)

            q = norm_matmul(h, xa_norm_g[l], wq, l, out_dtype=BF16)
            kv = norm_matmul(mem_b, mem_norm_g, wkv, l, out_dtype=BF16)
            att = cross_attention(q, kv)
            h = matmul_residual(att, wo, l, h, tm=512, tn=D_MODEL)

            act = ffn_up(h, ffn_norm_g[l], w_up, l, ffn_conv_w[l], ffn_conv_b[l])
            if l + 1 < depth:
                h = matmul_residual(act, w_down, l, h, tm=1024, tn=512)
            else:
                h = matmul_residual_norm(act, w_down, l, h, final_norm_g)
        outs.append(h)
    return jnp.stack(outs, axis=0)
```

```python
import functools

import jax
import jax.numpy as jnp
from jax import lax
from jax.experimental import pallas as pl
from jax.experimental.pallas import tpu as pltpu

F32 = jnp.float32
BF16 = jnp.bfloat16

D_MODEL = 2048
HG_HEADS = 16
HG_KDIM = 128
HG_VDIM = 128
D_HG = HG_HEADS * HG_KDIM
M2_HEADS = 32
M2_HEADDIM = 64
D_M2 = M2_HEADS * M2_HEADDIM
M2_GROUPS = 8
M2_DSTATE = 128
M2_CONV = 4
M2_GROUP_W = D_M2 // M2_GROUPS
M2_HPG = M2_HEADS // M2_GROUPS
D_BC = M2_GROUPS * M2_DSTATE
CHUNK = 64
HG_SUB = 32
HG_HPS = 4
XA_HEADS = 4
XA_HEADDIM = D_MODEL // XA_HEADS
D_FF = 5632
FFN_CONV = 3
NORM_EPS = 1e-6
M2_NORM_EPS = 1e-5
LB_FLOOR = 1e-30
HG_SAFE_DECAY = 80.0

IN_DT = 4 * D_HG + 2 * D_M2 + 2 * D_BC
IN_GHG = IN_DT + M2_HEADS
OFF_HQ = 0
OFF_HF = OFF_HQ + D_HG
OFF_HI = OFF_HF + D_HG
OFF_HOG = OFF_HI + D_HG
OFF_MZ = OFF_HOG + D_HG
OFF_MX = OFF_MZ + D_M2
OFF_MB = OFF_MX + D_M2
OFF_MC = OFF_MB + D_BC
OFF_GHG = OFF_MC + D_BC
OFF_GM2 = OFF_GHG + D_MODEL
DT_PAD = 128

HALO = 8
VMEM_LIMIT = 56 * 1024 * 1024


def _cparams(n_axes):
    return pltpu.CompilerParams(
        dimension_semantics=("arbitrary",) * n_axes,
        vmem_limit_bytes=VMEM_LIMIT)


def _rms(x, g, eps):
    return x * lax.rsqrt(jnp.mean(x * x, axis=-1, keepdims=True) + eps) * g


def _sigmoid(x):
    return 0.5 * jnp.tanh(0.5 * x) + 0.5


def _silu(x):
    hx = 0.5 * x
    return hx * jnp.tanh(hx) + hx


def _softplus(x):
    return jnp.maximum(x, 0.0) + jnp.log1p(jnp.exp(-jnp.abs(x)))


def _dot(a, b):
    return jnp.dot(a, b, preferred_element_type=F32)


def _dot_nt(a, b):
    return lax.dot_general(a, b, (((1,), (1,)), ((), ())), preferred_element_type=F32)


def _dot_tn(a, b):
    return lax.dot_general(a, b, (((0,), (0,)), ((), ())), preferred_element_type=F32)


def _split3(x):
    hi = x.astype(BF16)
    r1 = x - hi.astype(F32)
    mid = r1.astype(BF16)
    lo = (r1 - mid.astype(F32)).astype(BF16)
    return hi, mid, lo


def _seg_cumsum(x, seg):
    row = lax.broadcasted_iota(jnp.int32, x.shape, 0) % seg
    k = 1
    while k < seg:
        x = x + jnp.where(row >= k, pltpu.roll(x, k, axis=0), 0.0)
        k *= 2
    return x


def _wspec(k, tn, layer, col0=0):
    return pl.BlockSpec((None, k, tn), lambda i, j: (layer, 0, col0 + j))


def _norm_matmul_kernel(a_ref, g_ref, b_ref, o_ref, an_ref):
    @pl.when(pl.program_id(1) == 0)
    def _():
        an_ref[...] = _rms(a_ref[...], g_ref[...], NORM_EPS).astype(BF16)

    o_ref[...] = _dot(an_ref[...], b_ref[...]).astype(o_ref.dtype)


def norm_matmul(a, g, w, layer, *, out_dtype, tm=1024, tn=1024):
    m, k = a.shape
    n = w.shape[2]
    tm = min(tm, m)
    tn = min(tn, n)
    return pl.pallas_call(
        _norm_matmul_kernel,
        grid=(m // tm, n // tn),
        in_specs=[
            pl.BlockSpec((tm, k), lambda i, j: (i, 0)),
            pl.BlockSpec((1, k), lambda i, j: (0, 0)),
            _wspec(k, tn, layer),
        ],
        out_specs=pl.BlockSpec((tm, tn), lambda i, j: (i, j)),
        out_shape=jax.ShapeDtypeStruct((m, n), out_dtype),
        scratch_shapes=[pltpu.VMEM((tm, k), BF16)],
        compiler_params=_cparams(2),
        name="norm_matmul",
    )(a, g.reshape(1, k), w)


def _in_proj_kernel(a_ref, g_ref, b_ref, wdt_ref, bias_ref, alog_ref, o_ref, dt_ref, cum_ref, an_ref):
    @pl.when(pl.program_id(1) == 0)
    def _():
        an = _rms(a_ref[...], g_ref[...], NORM_EPS).astype(BF16)
        an_ref[...] = an
        dt = _softplus(_dot(an, wdt_ref[...]) + bias_ref[...])
        dt_ref[...] = dt
        cum_ref[...] = _seg_cumsum(dt * (-jnp.exp(alog_ref[...])), CHUNK)

    o_ref[...] = _dot(an_ref[...], b_ref[...])


def in_proj(a, g, w, w_dt, layer, dt_bias, a_log, *, tm=1024, tn=1536):
    m, k = a.shape
    n = w.shape[2]
    tm = min(tm, m)

    def pad(p):
        return jnp.pad(p, (0, DT_PAD - M2_HEADS)).reshape(1, DT_PAD)

    small = jax.ShapeDtypeStruct((m, DT_PAD), F32)
    small_spec = pl.BlockSpec((tm, DT_PAD), lambda i, j: (i, 0))
    return pl.pallas_call(
        _in_proj_kernel,
        grid=(m // tm, n // tn),
        in_specs=[
            pl.BlockSpec((tm, k), lambda i, j: (i, 0)),
            pl.BlockSpec((1, k), lambda i, j: (0, 0)),
            _wspec(k, tn, layer),
            pl.BlockSpec((None, k, DT_PAD), lambda i, j: (layer, 0, 0)),
            pl.BlockSpec((1, DT_PAD), lambda i, j: (0, 0)),
            pl.BlockSpec((1, DT_PAD), lambda i, j: (0, 0)),
        ],
        out_specs=[pl.BlockSpec((tm, tn), lambda i, j: (i, j)), small_spec, small_spec],
        out_shape=[jax.ShapeDtypeStruct((m, n), F32), small, small],
        scratch_shapes=[pltpu.VMEM((tm, k), BF16)],
        compiler_params=_cparams(2),
        name="in_proj",
    )(a, g.reshape(1, k), w, w_dt, pad(dt_bias), pad(a_log))


def _ffn_up_kernel(a_ref, g_ref, wg_ref, wu_ref, cw_ref, cb_ref, o_ref, an_ref, halo_ref, ext_ref,
                   *, tm):
    i = pl.program_id(0)
    j = pl.program_id(1)

    @pl.when(jnp.logical_and(i == 0, j == 0))
    def _():
        halo_ref[...] = jnp.zeros_like(halo_ref)

    @pl.when(j == 0)
    def _():
        an_ref[...] = _rms(a_ref[...], g_ref[...], NORM_EPS).astype(BF16)

    an = an_ref[...]
    ext_ref[0:HALO, :] = halo_ref[j]
    ext_ref[HALO:HALO + tm, :] = _dot(an, wg_ref[...])
    w = cw_ref[...]
    acc = cb_ref[...] + w[FFN_CONV - 1:FFN_CONV, :] * ext_ref[HALO:HALO + tm, :]
    for t in range(1, FFN_CONV):
        acc = acc + w[FFN_CONV - 1 - t:FFN_CONV - t, :] * ext_ref[HALO - t:HALO - t + tm, :]
    halo_ref[j] = ext_ref[tm:tm + HALO, :]
    gelu = 0.5 * acc * (1.0 + lax.erf(acc * (2.0 ** -0.5)))
    o_ref[...] = (gelu * _dot(an, wu_ref[...])).astype(o_ref.dtype)


def ffn_up(a, g, w_up, layer, conv_w, conv_b, *, tm=1024, tn=512):
    m, k = a.shape
    tm = min(tm, m)
    n_col = D_FF // tn
    kern = functools.partial(_ffn_up_kernel, tm=tm)
    return pl.pallas_call(
        kern,
        grid=(m // tm, n_col),
        in_specs=[
            pl.BlockSpec((tm, k), lambda i, j: (i, 0)),
            pl.BlockSpec((1, k), lambda i, j: (0, 0)),
            _wspec(k, tn, layer),
            _wspec(k, tn, layer, n_col),
            pl.BlockSpec((FFN_CONV, tn), lambda i, j: (0, j)),
            pl.BlockSpec((1, tn), lambda i, j: (0, j)),
        ],
        out_specs=pl.BlockSpec((tm, tn), lambda i, j: (i, j)),
        out_shape=jax.ShapeDtypeStruct((m, D_FF), BF16),
        scratch_shapes=[pltpu.VMEM((tm, k), BF16),
                        pltpu.VMEM((n_col, HALO, tn), F32),
                        pltpu.VMEM((tm + HALO, tn), F32)],
        compiler_params=_cparams(2),
        name="ffn_up",
    )(a, g.reshape(1, k), w_up, w_up, conv_w, conv_b.reshape(1, D_FF))


def _matmul_res_kernel(a_ref, b_ref, r_ref, o_ref):
    o_ref[...] = r_ref[...] + _dot(a_ref[...], b_ref[...])


def matmul_residual(a, w, layer, r, *, tm, tn):
    m, k = a.shape
    n = w.shape[2]
    tm = min(tm, m)
    tn = min(tn, n)
    return pl.pallas_call(
        _matmul_res_kernel,
        grid=(m // tm, n // tn),
        in_specs=[
            pl.BlockSpec((tm, k), lambda i, j: (i, 0)),
            _wspec(k, tn, layer),
            pl.BlockSpec((tm, tn), lambda i, j: (i, j)),
        ],
        out_specs=pl.BlockSpec((tm, tn), lambda i, j: (i, j)),
        out_shape=jax.ShapeDtypeStruct((m, n), F32),
        compiler_params=_cparams(2),
        name="matmul_residual",
    )(a, w, r)


def _matmul_res_norm_kernel(a_ref, b_ref, r_ref, g_ref, o_ref):
    o_ref[...] = _rms(r_ref[...] + _dot(a_ref[...], b_ref[...]), g_ref[...], NORM_EPS)


def matmul_residual_norm(a, w, layer, r, g, *, tm=512):
    m, k = a.shape
    n = w.shape[2]
    tm = min(tm, m)
    return pl.pallas_call(
        _matmul_res_norm_kernel,
        grid=(m // tm,),
        in_specs=[
            pl.BlockSpec((tm, k), lambda i: (i, 0)),
            pl.BlockSpec((None, k, n), lambda i: (layer, 0, 0), pipeline_mode=pl.Buffered(1)),
            pl.BlockSpec((tm, n), lambda i: (i, 0)),
            pl.BlockSpec((1, n), lambda i: (0, 0)),
        ],
        out_specs=pl.BlockSpec((tm, n), lambda i: (i, 0)),
        out_shape=jax.ShapeDtypeStruct((m, n), F32),
        compiler_params=_cparams(1),
        name="matmul_residual_norm",
    )(a, w, r, g.reshape(1, n))


def _merge_kernel(a1_ref, b1_ref, a2_ref, b2_ref, g1_ref, g2_ref, o_ref):
    y1 = _dot(a1_ref[...], b1_ref[...])
    y2 = _dot(a2_ref[...], b2_ref[...])
    o_ref[...] = (_sigmoid(g1_ref[...]) * y1 + _sigmoid(g2_ref[...]) * y2).astype(o_ref.dtype)


def merge_branches(o_hg, w_hg, o_m2, w_m2, layer, proj, *, tm=512, tn=1024):
    m, k = o_hg.shape
    n = w_hg.shape[2]
    tm = min(tm, m)
    g1 = OFF_GHG // tn
    g2 = OFF_GM2 // tn
    wspec = pl.BlockSpec((None, k, tn), lambda j, i: (layer, 0, j))
    return pl.pallas_call(
        _merge_kernel,
        grid=(n // tn, m // tm),
        in_specs=[
            pl.BlockSpec((tm, k), lambda j, i: (i, 0)),
            wspec,
            pl.BlockSpec((tm, k), lambda j, i: (i, 0)),
            wspec,
            pl.BlockSpec((tm, tn), lambda j, i: (i, g1 + j)),
            pl.BlockSpec((tm, tn), lambda j, i: (i, g2 + j)),
        ],
        out_specs=pl.BlockSpec((tm, tn), lambda j, i: (i, j)),
        out_shape=jax.ShapeDtypeStruct((m, n), BF16),
        compiler_params=_cparams(2),
        name="merge_branches",
    )(o_hg, w_hg, o_m2, w_m2, proj, proj)


def _hgrn2_kernel(q_ref, f_ref, i_ref, og_ref, lbl_ref, ng_ref, o_ref,
                  st_ref, sc_ref, stb_ref, bl_s, q_s, k_s, *, layer, tb):
    n_sub = tb // HG_SUB
    width = HG_HPS * HG_KDIM

    @pl.when(pl.program_id(1) == 0)
    def _():
        st_ref[...] = jnp.zeros_like(st_ref)

    logits = lbl_ref[...]
    e = jnp.exp(logits - jnp.max(logits, axis=0, keepdims=True))
    p = e / jnp.sum(e, axis=0, keepdims=True)
    lb = jnp.zeros((1, width), F32)
    for i in range(1, layer + 1):
        lb = lb + p[i:i + 1, :]
    lbf = jnp.maximum(lb, LB_FLOOR)

    z = f_ref[...]
    t = jnp.exp(-jnp.abs(z))
    r = 1.0 / (1.0 + t)
    pos = z >= 0.0
    log_f = jnp.log(jnp.where(pos, 1.0 + lbf * t, t + lbf) * r)
    kk = (1.0 - lb) * jnp.where(pos, t, 1.0) * r
    q = q_ref[...] * (HG_KDIM ** -0.5)
    v = i_ref[...].astype(BF16)

    bl = _seg_cumsum(log_f, HG_SUB)
    qt = (q * jnp.exp(bl)).astype(BF16)
    kinc = (kk * jnp.exp(-bl)).astype(BF16)
    worst = jnp.max(-bl)

    def blk(a, n, h):
        return a[n * HG_SUB:(n + 1) * HG_SUB, h * HG_KDIM:(h + 1) * HG_KDIM]

    for h in range(HG_HPS):
        for n in range(n_sub):
            sc_ref[h * n_sub + n] = _dot_nt(blk(qt, n, h), blk(kinc, n, h))

    for h in range(HG_HPS):
        st = st_ref[h]
        for n in range(n_sub):
            blc = blk(bl, n, h)
            b_last = blc[HG_SUB - 1:HG_SUB, :]
            kdec = (blk(kk, n, h) * jnp.exp(b_last - blc)).astype(BF16)
            stb_ref[h * n_sub + n] = st.astype(BF16)
            st = st * jnp.exp(b_last) + _dot_tn(blk(v, n, h), kdec)
        st_ref[h] = st

    col_id = lax.broadcasted_iota(jnp.int32, (HG_SUB, HG_SUB), 1)
    tri = lax.broadcasted_iota(jnp.int32, (HG_SUB, HG_SUB), 0) >= col_id

    @pl.when(worst > HG_SAFE_DECAY)
    def _():
        for h in range(HG_HPS):
            cols = slice(h * HG_KDIM, (h + 1) * HG_KDIM)
            bl_s[h * tb:(h + 1) * tb, :] = bl[:, cols]
            q_s[h * tb:(h + 1) * tb, :] = q[:, cols]
            k_s[h * tb:(h + 1) * tb, :] = kk[:, cols]

        def body(n, carry):
            off = pl.multiple_of(n * HG_SUB, HG_SUB)
            blc = bl_s[pl.ds(off, HG_SUB), :]
            qc = q_s[pl.ds(off, HG_SUB), :]
            kc = k_s[pl.ds(off, HG_SUB), :]
            sc = jnp.zeros((HG_SUB, HG_SUB), F32)
            for s in range(HG_SUB):
                d = jnp.exp(jnp.minimum(blc - blc[s:s + 1, :], 0.0))
                col = jnp.sum(qc * d * kc[s:s + 1, :], axis=-1, keepdims=True)
                sc = jnp.where(col_id == s, col, sc)
            sc_ref[n] = sc
            return carry

        lax.fori_loop(0, HG_HPS * n_sub, body, 0)

    for h in range(HG_HPS):
        cols = slice(h * HG_KDIM, (h + 1) * HG_KDIM)
        outs = []
        for n in range(n_sub):
            sc = jnp.where(tri, sc_ref[h * n_sub + n], 0.0).astype(BF16)
            outs.append(_dot(sc, blk(v, n, h)) + _dot_nt(blk(qt, n, h), stb_ref[h * n_sub + n]))
        o = jnp.concatenate(outs, axis=0) * _sigmoid(og_ref[:, cols])
        o_ref[:, cols] = _rms(o, ng_ref[:, cols], NORM_EPS).astype(o_ref.dtype)


def hgrn2_branch(proj, lb_logits, norm_g, layer, *, tb=1024):
    s = proj.shape[0]
    tb = min(tb, s)
    depth = lb_logits.shape[0]
    n_sub = tb // HG_SUB
    width = HG_HPS * HG_KDIM
    kern = functools.partial(_hgrn2_kernel, layer=layer, tb=tb)

    def col(off):
        base = off // width
        return pl.BlockSpec((tb, width), lambda h, i: (i, base + h))

    return pl.pallas_call(
        kern,
        grid=(HG_HEADS // HG_HPS, s // tb),
        in_specs=[
            col(OFF_HQ), col(OFF_HF), col(OFF_HI), col(OFF_HOG),
            pl.BlockSpec((depth, width), lambda h, i: (0, h)),
            pl.BlockSpec((1, width), lambda h, i: (0, h)),
        ],
        out_specs=pl.BlockSpec((tb, width), lambda h, i: (i, h)),
        out_shape=jax.ShapeDtypeStruct((s, D_HG), BF16),
        scratch_shapes=[pltpu.VMEM((HG_HPS, HG_VDIM, HG_KDIM), F32),
                        pltpu.VMEM((HG_HPS * n_sub, HG_SUB, HG_SUB), F32),
                        pltpu.VMEM((HG_HPS * n_sub, HG_VDIM, HG_KDIM), BF16),
                        pltpu.VMEM((HG_HPS * tb, HG_KDIM), F32),
                        pltpu.VMEM((HG_HPS * tb, HG_KDIM), F32),
                        pltpu.VMEM((HG_HPS * tb, HG_KDIM), F32)],
        compiler_params=_cparams(2),
        name="hgrn2_branch",
    )(proj, proj, proj, proj, lb_logits, norm_g.reshape(1, D_HG))


def _mamba2_kernel(z_ref, x_ref, b_ref, c_ref, dt_ref, cum_ref, wx_ref, wb_ref, wc_ref,
                   bx_ref, bb_ref, bc_ref, dsk_ref, ng_ref,
                   o_ref, ext_ref, st_ref, stb_ref, y_ref, *, tb):
    g = pl.program_id(0)
    gw = M2_GROUP_W
    ns = M2_DSTATE
    w_all = gw + 2 * ns
    n_chunk = tb // CHUNK

    @pl.when(pl.program_id(1) == 0)
    def _():
        st_ref[...] = jnp.zeros_like(st_ref)
        ext_ref[0:HALO, :] = jnp.zeros((HALO, w_all), F32)

    ext_ref[HALO:HALO + tb, 0:gw] = x_ref[...]
    ext_ref[HALO:HALO + tb, gw:gw + ns] = b_ref[...]
    ext_ref[HALO:HALO + tb, gw + ns:w_all] = c_ref[...]
    w = jnp.concatenate([wx_ref[...], wb_ref[...], wc_ref[...]], axis=1)
    bias = jnp.concatenate([bx_ref[...], bb_ref[...], bc_ref[...]], axis=1)
    acc = bias + w[M2_CONV - 1:M2_CONV, :] * ext_ref[HALO:HALO + tb, :]
    for j in range(1, M2_CONV):
        acc = acc + w[M2_CONV - 1 - j:M2_CONV - j, :] * ext_ref[HALO - j:HALO - j + tb, :]
    ext_ref[0:HALO, :] = ext_ref[tb:tb + HALO, :]
    xbc = _silu(acc)
    xs = xbc[:, 0:gw]
    bm = xbc[:, gw:gw + ns].astype(BF16)
    cm = xbc[:, gw + ns:w_all].astype(BF16)

    r_id = lax.broadcasted_iota(jnp.int32, (DT_PAD, gw), 0)
    c_id = lax.broadcasted_iota(jnp.int32, (DT_PAD, gw), 1)
    expand = (r_id == g * M2_HPG + c_id // M2_HEADDIM).astype(BF16)
    d_parts = _split3(dt_ref[...])
    c_parts = _split3(cum_ref[...])
    dtx = _dot(d_parts[0], expand) + _dot(d_parts[1], expand) + _dot(d_parts[2], expand)
    cumx = _dot(c_parts[0], expand) + _dot(c_parts[1], expand) + _dot(c_parts[2], expand)
    xdt = xs * dtx
    ecum = jnp.exp(cumx)

    sr = lax.broadcasted_iota(jnp.int32, (HALO, DT_PAD), 0)
    sl = lax.broadcasted_iota(jnp.int32, (HALO, DT_PAD), 1)
    sel = jnp.logical_and(sr < M2_HPG, sl == g * M2_HPG + sr).astype(BF16)

    def rot(a):
        return jnp.concatenate([a[tb - CHUNK:, :], a[:tb - CHUNK, :]], axis=0)

    rows_all = _dot_nt(sel, c_parts[0]) + _dot_nt(sel, c_parts[1]) + _dot_nt(sel, c_parts[2])
    rows_rot = (_dot_nt(sel, rot(c_parts[0])) + _dot_nt(sel, rot(c_parts[1]))
                + _dot_nt(sel, rot(c_parts[2])))

    pair_w = 2 * CHUNK
    low_half = lax.broadcasted_iota(jnp.int32, (1, pair_w), 1) < CHUNK
    tri = (lax.broadcasted_iota(jnp.int32, (CHUNK, pair_w), 0)
           >= lax.broadcasted_iota(jnp.int32, (CHUNK, pair_w), 1) % CHUNK)
    head_of_col = lax.broadcasted_iota(jnp.int32, (1, gw), 1) // M2_HEADDIM

    def rows(a, n):
        return a[n * CHUNK:(n + 1) * CHUNK, :]

    for n in range(n_chunk):
        cb4 = _dot_nt(rows(cm, n), jnp.concatenate([rows(bm, n)] * M2_HPG, axis=0))
        cumc = rows(cumx, n)
        lparts = []
        for p_ in range(M2_HPG // 2):
            if n % 2 == 0:
                a0 = n * CHUNK
                rp = jnp.where(low_half, rows_all[2 * p_:2 * p_ + 1, a0:a0 + pair_w],
                               rows_rot[2 * p_ + 1:2 * p_ + 2, a0:a0 + pair_w])
            else:
                a_lo = ((n + 1) * CHUNK) % tb
                a_hi = (n - 1) * CHUNK
                rp = jnp.where(low_half, rows_rot[2 * p_:2 * p_ + 1, a_lo:a_lo + pair_w],
                               rows_all[2 * p_ + 1:2 * p_ + 2, a_hi:a_hi + pair_w])
            diff = cumc[:, p_ * pair_w:(p_ + 1) * pair_w] - rp
            lparts.append(jnp.where(tri, jnp.exp(jnp.where(tri, diff, 0.0)), 0.0))
        lhs = (cb4 * jnp.concatenate(lparts, axis=1)).astype(BF16)
        xdtb = rows(xdt, n).astype(BF16)
        rhs = jnp.concatenate(
            [jnp.where(head_of_col == h, xdtb, jnp.zeros_like(xdtb)) for h in range(M2_HPG)], axis=0)
        y_ref[n * CHUNK:(n + 1) * CHUNK, :] = _dot(lhs, rhs)

    st = st_ref[...]
    for n in range(n_chunk):
        cumc = rows(cumx, n)
        last = cumc[CHUNK - 1:CHUNK, :]
        wdec = (rows(xdt, n) * jnp.exp(last - cumc)).astype(BF16)
        stb_ref[n] = st.astype(BF16)
        st = st * jnp.exp(last) + _dot_tn(rows(bm, n), wdec)
    st_ref[...] = st

    for n in range(n_chunk):
        y_ref[n * CHUNK:(n + 1) * CHUNK, :] += _dot(rows(cm, n), stb_ref[n]) * rows(ecum, n)

    y = (y_ref[...] + dsk_ref[...] * xs) * _silu(z_ref[...])
    o_ref[...] = _rms(y, ng_ref[...], M2_NORM_EPS).astype(o_ref.dtype)


def mamba2_branch(proj, dt, cum, conv_w, conv_b, d_x, norm_g, *, tb=1024):
    s = proj.shape[0]
    tb = min(tb, s)
    kern = functools.partial(_mamba2_kernel, tb=tb)
    gw = M2_GROUP_W
    ns = M2_DSTATE
    w_all = gw + 2 * ns

    def colw(off, width):
        base = off // width
        return pl.BlockSpec((tb, width), lambda g, i: (i, base + g))

    def par(rows, off, width):
        base = off // width
        return pl.BlockSpec((rows, width), lambda g, i: (0, base + g))

    conv_b = conv_b.reshape(1, -1)
    return pl.pallas_call(
        kern,
        grid=(M2_GROUPS, s // tb),
        in_specs=[
            colw(OFF_MZ, gw), colw(OFF_MX, gw), colw(OFF_MB, ns), colw(OFF_MC, ns),
            pl.BlockSpec((tb, DT_PAD), lambda g, i: (i, 0)),
            pl.BlockSpec((tb, DT_PAD), lambda g, i: (i, 0)),
            par(M2_CONV, 0, gw), par(M2_CONV, D_M2, ns), par(M2_CONV, D_M2 + D_BC, ns),
            par(1, 0, gw), par(1, D_M2, ns), par(1, D_M2 + D_BC, ns),
            par(1, 0, gw), par(1, 0, gw),
        ],
        out_specs=pl.BlockSpec((tb, gw), lambda g, i: (i, g)),
        out_shape=jax.ShapeDtypeStruct((s, D_M2), BF16),
        scratch_shapes=[pltpu.VMEM((tb + HALO, w_all), F32),
                        pltpu.VMEM((ns, gw), F32),
                        pltpu.VMEM((tb // CHUNK, ns, gw), BF16),
                        pltpu.VMEM((tb, gw), F32)],
        compiler_params=_cparams(2),
        name="mamba2_branch",
    )(proj, proj, proj, proj, dt, cum, conv_w, conv_w, conv_w, conv_b, conv_b, conv_b,
      d_x, norm_g.reshape(1, D_M2))


def _xattn_layer_kernel(h_ref, g_ref, wq_ref, k_ref, v_ref, wo_ref, o_ref, att_ref):
    h = h_ref[...]
    u = _rms(h, g_ref[...], NORM_EPS).astype(BF16)
    q = _dot(u, wq_ref[...]).astype(BF16)
    scale = XA_HEADDIM ** -0.5
    for hd in range(XA_HEADS):
        lo = hd * XA_HEADDIM
        hi = lo + XA_HEADDIM
        sc = _dot_nt(q[:, lo:hi], k_ref[:, lo:hi]) * scale
        e = jnp.exp(sc - jnp.max(sc, axis=-1, keepdims=True))
        p = e / jnp.sum(e, axis=-1, keepdims=True)
        att_ref[:, lo:hi] = _dot(p.astype(BF16), v_ref[:, lo:hi]).astype(BF16)
    o_ref[...] = h + _dot(att_ref[...], wo_ref[...])


def cross_attention_layer(h, g, wq, kv, wo, layer, *, tm=512):
    s, d = h.shape
    n_mem = kv.shape[0]
    tm = min(tm, s)

    def resident(w):
        return pl.BlockSpec((None, d, d), lambda i: (layer, 0, 0), pipeline_mode=pl.Buffered(1))

    return pl.pallas_call(
        _xattn_layer_kernel,
        grid=(s // tm,),
        in_specs=[
            pl.BlockSpec((tm, d), lambda i: (i, 0)),
            pl.BlockSpec((1, d), lambda i: (0, 0)),
            resident(wq),
            pl.BlockSpec((n_mem, d), lambda i: (0, 0)),
            pl.BlockSpec((n_mem, d), lambda i: (0, 1)),
            resident(wo),
        ],
        out_specs=pl.BlockSpec((tm, d), lambda i: (i, 0)),
        out_shape=jax.ShapeDtypeStruct((s, d), F32),
        scratch_shapes=[pltpu.VMEM((tm, d), BF16)],
        compiler_params=_cparams(1),
        name="cross_attention_layer",
    )(h, g.reshape(1, d), wq, kv, kv, wo)


def _per_head_lanes(p):
    return jnp.repeat(p, M2_HEADDIM).reshape(1, D_M2)


def kernel(x, mem, mix_norm_g, w_in, hg_lb_logits, hg_norm_g, m2_conv_w, m2_conv_b, m2_dt_bias, m2_A_log, m2_D, m2_norm_g, w_branch_hg, w_branch_m2, w_out, mem_norm_g, xa_norm_g, xa_wq, xa_wkv, xa_wo, ffn_norm_g, ffn_w_up, ffn_conv_w, ffn_conv_b, ffn_w_down, final_norm_g):
    bsz, seq, d = x.shape
    depth = w_in.shape[0]
    w_proj = jnp.concatenate([w_in[:, :, :IN_DT], w_in[:, :, IN_GHG:]], axis=2).astype(BF16)
    w_dt = w_in[:, :, IN_DT:IN_DT + DT_PAD].astype(BF16)
    w_bhg = w_branch_hg.astype(BF16)
    w_bm2 = w_branch_m2.astype(BF16)
    w_o = w_out.astype(BF16)
    wq = xa_wq.astype(BF16)
    wkv = xa_wkv.astype(BF16)
    wo = xa_wo.astype(BF16)
    w_up = ffn_w_up.astype(BF16)
    w_down = ffn_w_down.astype(BF16)
    outs = []
    for b in range(bsz):
        h = x[b]
        mem_b = mem[b]
        for l in range(depth):
            proj, dt, cum = in_proj(h, mix_norm_g[l], w_proj, w_dt, l, m2_dt_bias[l], m2_A_log[l])
            o_hg = hgrn2_branch(proj, hg_lb_logits, hg_norm_g[l], l)
            o_m2 = mamba2_branch(proj, dt, cum, m2_conv_w[l], m2_conv_b[l],
                                 _per_head_lanes(m2_D[l]), m2_norm_g[l])
            merged = merge_branches(o_hg, w_bhg, o_m2, w_bm2, l, proj)
            h = matmul_residual(merged, w_o, l, h, tm=512, tn=D_MODEL)

            kv = norm_matmul(mem_b, mem_norm_g, wkv, l, out_dtype=BF16)
            h = cross_attention_layer(h, xa_norm_g[l], wq, kv, wo, l)

            act = ffn_up(h, ffn_norm_g[l], w_up, l, ffn_conv_w[l], ffn_conv_b[l])
            if l + 1 < depth:
                h = matmul_residual(act, w_down, l, h, tm=1024, tn=512)
            else:
                h = matmul_residual_norm(act, w_down, l, h, final_norm_g)
        outs.append(h)
    return jnp.stack(outs, axis=0)
```
